```python
import jax, jax.numpy as jnp
from jax import lax
import numpy as np

D_MODEL = 1024
BATCH = 16
SEQ = 4096
DEPTH = 4

GRID_W = 64
CHUNK = 128
Q_BLOCK = 128
N_HEADS = 8
N_KV_HEADS = 2
HEAD_DIM = 64
GQA_GROUP = N_HEADS // N_KV_HEADS
ATTN_WIDTH = N_HEADS * HEAD_DIM
KV_WIDTH = N_KV_HEADS * HEAD_DIM
SG_HEADS = 8
SG_HEAD_DIM = 64
SG_WIDTH = SG_HEADS * SG_HEAD_DIM
MIX_WIDTH = ATTN_WIDTH + SG_WIDTH
IN_WIDTH = ATTN_WIDTH + 2 * KV_WIDTH + 2 * SG_WIDTH
D_FF = 2816
CONV_W = 3
ROPE_THETA = 10000.0
EPS = 1e-6

kernel_name = "hybrid_gmlp_gqa_axialrope_convffn"


def rms_norm(x, g):
    xf = x.astype(jnp.float32)
    y = xf * lax.rsqrt(jnp.mean(xf * xf, axis=-1, keepdims=True) + EPS)
    return (y * g.astype(jnp.float32)).astype(x.dtype)


def axial_rope_tables(seq_len):
    rows = seq_len // GRID_W
    row = jnp.repeat(jnp.arange(rows, dtype=jnp.float32), GRID_W)
    col = jnp.tile(jnp.arange(GRID_W, dtype=jnp.float32), rows)
    axis_dim = HEAD_DIM // 2
    inv_freq = 1.0 / (ROPE_THETA ** (jnp.arange(0, axis_dim, 2, dtype=jnp.float32) / axis_dim))
    ang_r = row[:, None] * inv_freq[None, :]
    ang_c = col[:, None] * inv_freq[None, :]
    return jnp.cos(ang_r), jnp.sin(ang_r), jnp.cos(ang_c), jnp.sin(ang_c)


def rotate(x, cos, sin):
    n = x.shape[-1] // 2
    x1, x2 = x[..., :n], x[..., n:]
    c = cos[None, :, None, :].astype(x.dtype)
    s = sin[None, :, None, :].astype(x.dtype)
    return jnp.concatenate([x1 * c - x2 * s, x2 * c + x1 * s], axis=-1)


def axial_rope(x, tables):
    cr, sr, cc, sc = tables
    half = HEAD_DIM // 2
    return jnp.concatenate([rotate(x[..., :half], cr, sr), rotate(x[..., half:], cc, sc)], axis=-1)


def gqa_attention(q, k, v):
    b, s, _, dh = q.shape
    nblk = s // Q_BLOCK
    qb = q.reshape(b, nblk, Q_BLOCK, N_KV_HEADS, GQA_GROUP, dh)
    qb = jnp.moveaxis(qb, 1, 0)
    scale = HEAD_DIM ** -0.5

    def one_block(qblk):
        sc = jnp.einsum('bqkgd,bskd->bkgqs', qblk, k).astype(jnp.float32) * scale
        p = jax.nn.softmax(sc, axis=-1).astype(v.dtype)
        return jnp.einsum('bkgqs,bskd->bqkgd', p, v)

    o = lax.map(one_block, qb)
    return jnp.moveaxis(o, 0, 1).reshape(b, s, ATTN_WIDTH)


def spatial_gating(u, vv, sg_norm_g, sg_w, sg_b):
    b, s, _ = u.shape
    nc = s // CHUNK
    vv = rms_norm(vv.reshape(b, s, SG_HEADS, SG_HEAD_DIM), sg_norm_g)
    vv = vv.reshape(b, nc, CHUNK, SG_HEADS, SG_HEAD_DIM)
    mixed = jnp.einsum('hpq,bnqhd->bnphd', sg_w, vv) + sg_b.T[:, :, None]
    out = u.reshape(b, nc, CHUNK, SG_HEADS, SG_HEAD_DIM) * mixed
    return out.reshape(b, s, SG_WIDTH)


def depthwise_conv(h, w, bias):
    c = h.shape[-1]
    y = lax.conv_general_dilated(h, w[:, None, :].astype(h.dtype), window_strides=(1,),
                                 padding=[(CONV_W // 2, CONV_W // 2)],
                                 dimension_numbers=('NWC', 'WIO', 'NWC'),
                                 feature_group_count=c)
    return y + bias


def setup_inputs(seed: int = 0) -> dict:
    key = jax.random.key(seed)
    ks = jax.random.split(key, 17)
    f32 = jnp.float32
    nrm = lambda k, shape, scale: jax.random.normal(k, shape, f32) * scale
    gain = lambda k, shape: 1.0 + 0.05 * jax.random.normal(k, shape, f32)
    return {
        "x": jax.random.normal(ks[0], (BATCH, SEQ, D_MODEL), f32),
        "attn_norm_g": gain(ks[1], (DEPTH, D_MODEL)),
        "w_in": nrm(ks[2], (DEPTH, D_MODEL, IN_WIDTH), D_MODEL ** -0.5),
        "q_norm_g": gain(ks[3], (DEPTH, HEAD_DIM)),
        "k_norm_g": gain(ks[4], (DEPTH, HEAD_DIM)),
        "sg_norm_g": gain(ks[5], (DEPTH, SG_HEADS, SG_HEAD_DIM)),
        "sg_w": nrm(ks[6], (DEPTH, SG_HEADS, CHUNK, CHUNK), 0.5 * CHUNK ** -0.5),
        "sg_b": gain(ks[7], (DEPTH, SG_HEADS, CHUNK)),
        "attn_out_g": gain(ks[8], (DEPTH, ATTN_WIDTH)),
        "sg_out_g": gain(ks[9], (DEPTH, SG_WIDTH)),
        "w_o": nrm(ks[10], (DEPTH, MIX_WIDTH, D_MODEL), MIX_WIDTH ** -0.5),
        "ffn_norm_g": gain(ks[11], (DEPTH, D_MODEL)),
        "w_up": nrm(ks[12], (DEPTH, D_MODEL, 2 * D_FF), D_MODEL ** -0.5),
        "conv_w": nrm(ks[13], (DEPTH, CONV_W, 2 * D_FF), CONV_W ** -0.5),
        "conv_b": nrm(ks[14], (DEPTH, 2 * D_FF), 0.02),
        "w_down": nrm(ks[15], (DEPTH, D_FF, D_MODEL), D_FF ** -0.5),
    }


def reference(x, attn_norm_g, w_in, q_norm_g, k_norm_g, sg_norm_g, sg_w, sg_b,
              attn_out_g, sg_out_g, w_o, ffn_norm_g, w_up, conv_w, conv_b, w_down):
    b, s, _ = x.shape
    tables = axial_rope_tables(s)
    split_at = [ATTN_WIDTH, ATTN_WIDTH + KV_WIDTH, ATTN_WIDTH + 2 * KV_WIDTH,
                ATTN_WIDTH + 2 * KV_WIDTH + SG_WIDTH]
    for i in range(DEPTH):
        h = rms_norm(x, attn_norm_g[i])
        proj = h @ w_in[i]
        q, k, v, u, vv = jnp.split(proj, split_at, axis=-1)
        q = rms_norm(q.reshape(b, s, N_HEADS, HEAD_DIM), q_norm_g[i])
        k = rms_norm(k.reshape(b, s, N_KV_HEADS, HEAD_DIM), k_norm_g[i])
        v = v.reshape(b, s, N_KV_HEADS, HEAD_DIM)
        q = axial_rope(q, tables)
        k = axial_rope(k, tables)
        attn_o = gqa_attention(q, k, v)
        sg_o = spatial_gating(jax.nn.gelu(u), jax.nn.gelu(vv),
                              sg_norm_g[i], sg_w[i], sg_b[i])
        merged = jnp.concatenate([rms_norm(attn_o, attn_out_g[i]),
                                  rms_norm(sg_o, sg_out_g[i])], axis=-1)
        x = x + merged @ w_o[i]
        h = rms_norm(x, ffn_norm_g[i]) @ w_up[i]
        h = depthwise_conv(h, conv_w[i], conv_b[i])
        gate, up = jnp.split(h, 2, axis=-1)
        x = x + (jax.nn.silu(gate) * up) @ w_down[i]
    return x
```

```python
import functools

import jax
import jax.numpy as jnp
import numpy as np
from jax import lax
from jax.experimental import pallas as pl
from jax.experimental.pallas import tpu as pltpu

GRID_W = 64
CHUNK = 128
N_HEADS = 8
N_KV_HEADS = 2
HEAD_DIM = 64
ATTN_WIDTH = N_HEADS * HEAD_DIM
KV_WIDTH = N_KV_HEADS * HEAD_DIM
SG_HEADS = 8
SG_WIDTH = SG_HEADS * HEAD_DIM
ROPE_THETA = 10000.0
EPS = 1e-6

LANES = 128
HALF = LANES // 2
VMEM_LIMIT = 56 * 1024 * 1024

F32 = jnp.float32
BF16 = jnp.bfloat16


def _rms(x, g):
    ms = jnp.mean(x * x, axis=-1, keepdims=True)
    return x * lax.rsqrt(ms + EPS) * g


def _low_half(shape):
    return lax.broadcasted_iota(jnp.int32, shape, len(shape) - 1) < HALF


def _head_rsqrt(xb):
    lo = _low_half(xb.shape)
    sq = xb * xb
    s_lo = jnp.sum(jnp.where(lo, sq, 0.0), axis=-1, keepdims=True)
    s_hi = jnp.sum(jnp.where(lo, 0.0, sq), axis=-1, keepdims=True)
    return lax.rsqrt(jnp.where(lo, s_lo, s_hi) * (1.0 / HEAD_DIM) + EPS)


def _gelu_tanh(x):
    c = np.float32(np.sqrt(2.0 / np.pi))
    return x * (0.5 * (1.0 + jnp.tanh(c * (x + 0.044715 * (x * x * x)))))


def _rope(xb, cos, sin_up, sin_dn):
    return xb * cos + pltpu.roll(xb, LANES - 16, 1) * sin_up + pltpu.roll(xb, 16, 1) * sin_dn


def _in_proj_kernel(x_ref, g_ref, w_ref, gq_ref, gk_ref, gsg_ref, cos_ref, sup_ref, sdn_ref,
                    wsg_ref, bsg_ref, gsgo_ref, q_ref, k_ref, v_ref, sg_ref, sgo_scr):
    tm = x_ref.shape[1]
    xn = _rms(x_ref[0], g_ref[...]).astype(BF16)
    proj = jnp.dot(xn, w_ref[...], preferred_element_type=F32)
    cos, sup, sdn = cos_ref[...], sup_ref[...], sdn_ref[...]

    for c in range(ATTN_WIDTH // LANES):
        qb = proj[:, c * LANES:(c + 1) * LANES]
        qb = qb * _head_rsqrt(qb) * gq_ref[:, c * LANES:(c + 1) * LANES]
        q_ref[0, :, c * LANES:(c + 1) * LANES] = _rope(qb, cos, sup, sdn).astype(BF16)

    kb = proj[:, ATTN_WIDTH:ATTN_WIDTH + KV_WIDTH]
    kb = _rope(kb * _head_rsqrt(kb) * gk_ref[...], cos, sup, sdn)
    k_ref[0, 0] = kb.astype(BF16)
    k_ref[0, 1] = pltpu.roll(kb, HALF, 1).astype(BF16)
    vb = proj[:, ATTN_WIDTH + KV_WIDTH:ATTN_WIDTH + 2 * KV_WIDTH]
    v_ref[0, 0] = vb.astype(BF16)
    v_ref[0, 1] = pltpu.roll(vb, HALF, 1).astype(BF16)

    u0 = ATTN_WIDTH + 2 * KV_WIDTH
    lo = _low_half((tm, LANES))
    for c in range(SG_WIDTH // LANES):
        ub = _gelu_tanh(proj[:, u0 + c * LANES:u0 + (c + 1) * LANES])
        vb = _gelu_tanh(proj[:, u0 + SG_WIDTH + c * LANES:u0 + SG_WIDTH + (c + 1) * LANES])
        vb = vb * _head_rsqrt(vb) * gsg_ref[:, c * LANES:(c + 1) * LANES]
        top = jnp.where(lo, vb, 0.0).astype(BF16)
        bot = jnp.where(lo, 0.0, vb).astype(BF16)
        for r in range(tm // CHUNK):
            rows = slice(r * CHUNK, (r + 1) * CHUNK)
            rhs = jnp.concatenate([top[rows], bot[rows]], axis=0)
            mixed = jnp.dot(wsg_ref[c], rhs, preferred_element_type=F32) + bsg_ref[c]
            sgo_scr[rows, c * LANES:(c + 1) * LANES] = ub[rows] * mixed
    sg_ref[0] = _rms(sgo_scr[...], gsgo_ref[...]).astype(BF16)


def _in_proj(x, g, w, gq, gk, gsg, cos, sup, sdn, wsg, bsg, gsgo, tm):
    b, s, d = x.shape
    n_in = w.shape[1]
    const = lambda *shape: pl.BlockSpec(shape, lambda bi, i: (0,) * len(shape))
    once = lambda *shape: pl.BlockSpec(shape, lambda bi, i: (0,) * len(shape),
                                       pipeline_mode=pl.Buffered(1))
    table = pl.BlockSpec((tm, LANES), lambda bi, i: (i, 0))
    return pl.pallas_call(
        _in_proj_kernel,
        grid=(b, s // tm),
        in_specs=[
            pl.BlockSpec((1, tm, d), lambda bi, i: (bi, i, 0)),
            const(1, d), once(d, n_in), const(1, ATTN_WIDTH), const(1, KV_WIDTH),
            const(1, SG_WIDTH), table, table, table,
            const(SG_WIDTH // LANES, CHUNK, 2 * CHUNK), const(SG_WIDTH // LANES, CHUNK, LANES),
            const(1, SG_WIDTH),
        ],
        out_specs=[
            pl.BlockSpec((1, tm, ATTN_WIDTH), lambda bi, i: (bi, i, 0)),
            pl.BlockSpec((1, 2, tm, KV_WIDTH), lambda bi, i: (bi, 0, i, 0)),
            pl.BlockSpec((1, 2, tm, KV_WIDTH), lambda bi, i: (bi, 0, i, 0)),
            pl.BlockSpec((1, tm, SG_WIDTH), lambda bi, i: (bi, i, 0)),
        ],
        out_shape=[
            jax.ShapeDtypeStruct((b, s, ATTN_WIDTH), BF16),
            jax.ShapeDtypeStruct((b, 2, s, KV_WIDTH), BF16),
            jax.ShapeDtypeStruct((b, 2, s, KV_WIDTH), BF16),
            jax.ShapeDtypeStruct((b, s, SG_WIDTH), BF16),
        ],
        scratch_shapes=[pltpu.VMEM((tm, SG_WIDTH), F32)],
        compiler_params=pltpu.CompilerParams(
            dimension_semantics=("parallel", "parallel"), vmem_limit_bytes=VMEM_LIMIT),
        name="in_proj",
    )(x, g, w, gq, gk, gsg, cos, sup, sdn, wsg, bsg, gsgo)


def _attention_kernel(q_ref, k_ref, v_ref, g_ref, o_ref, o_scr):
    tq = q_ref.shape[1]
    lo = _low_half((tq, LANES))
    heads_per_kv = N_HEADS // N_KV_HEADS
    for c in range(ATTN_WIDTH // LANES):
        qb = q_ref[0, :, c * LANES:(c + 1) * LANES]
        kv_head = (2 * c) // heads_per_kv
        halves = []
        for half in range(2):
            keep = lo if half == 0 else jnp.logical_not(lo)
            qm = jnp.where(keep, qb, jnp.zeros_like(qb))
            var = 0 if kv_head == half else 1
            s = lax.dot_general(qm, k_ref[0, var], (((1,), (1,)), ((), ())),
                                preferred_element_type=F32)
            p = jnp.exp(s - jnp.max(s, axis=-1, keepdims=True))
            l = jnp.sum(p, axis=-1, keepdims=True)
            a = jnp.dot(p.astype(BF16), v_ref[0, var], preferred_element_type=F32)
            halves.append(a / l)
        o_scr[:, c * LANES:(c + 1) * LANES] = jnp.where(lo, halves[0], halves[1])
    o_ref[0] = _rms(o_scr[...], g_ref[...]).astype(BF16)


def _attention(q, k2, v2, g, tq):
    b, s, _ = q.shape
    return pl.pallas_call(
        _attention_kernel,
        grid=(b, s // tq),
        in_specs=[
            pl.BlockSpec((1, tq, ATTN_WIDTH), lambda bi, i: (bi, i, 0)),
            pl.BlockSpec((1, 2, s, KV_WIDTH), lambda bi, i: (bi, 0, 0, 0)),
            pl.BlockSpec((1, 2, s, KV_WIDTH), lambda bi, i: (bi, 0, 0, 0)),
            pl.BlockSpec((1, ATTN_WIDTH), lambda bi, i: (0, 0)),
        ],
        out_specs=pl.BlockSpec((1, tq, ATTN_WIDTH), lambda bi, i: (bi, i, 0)),
        out_shape=jax.ShapeDtypeStruct((b, s, ATTN_WIDTH), BF16),
        scratch_shapes=[pltpu.VMEM((tq, ATTN_WIDTH), F32)],
        compiler_params=pltpu.CompilerParams(
            dimension_semantics=("parallel", "parallel"), vmem_limit_bytes=VMEM_LIMIT),
        name="attention",
    )(q, k2, v2, g)


def _out_proj_kernel(x_ref, a_ref, sg_ref, w_ref, o_ref):
    y = jnp.dot(a_ref[0], w_ref[:ATTN_WIDTH, :], preferred_element_type=F32)
    y = y + jnp.dot(sg_ref[0], w_ref[ATTN_WIDTH:, :], preferred_element_type=F32)
    o_ref[0] = x_ref[0] + y


def _out_proj(x, a, sg, w, tm):
    b, s, d = x.shape
    return pl.pallas_call(
        _out_proj_kernel,
        grid=(b, s // tm),
        in_specs=[
            pl.BlockSpec((1, tm, d), lambda bi, i: (bi, i, 0)),
            pl.BlockSpec((1, tm, ATTN_WIDTH), lambda bi, i: (bi, i, 0)),
            pl.BlockSpec((1, tm, SG_WIDTH), lambda bi, i: (bi, i, 0)),
            pl.BlockSpec(w.shape, lambda bi, i: (0, 0), pipeline_mode=pl.Buffered(1)),
        ],
        out_specs=pl.BlockSpec((1, tm, d), lambda bi, i: (bi, i, 0)),
        out_shape=jax.ShapeDtypeStruct((b, s, d), F32),
        compiler_params=pltpu.CompilerParams(
            dimension_semantics=("parallel", "parallel"), vmem_limit_bytes=VMEM_LIMIT),
        name="out_proj",
    )(x, a, sg, w)


HALO = 8


def _ffn_kernel(x_ref, xp_ref, xn_ref, g_ref, wup_ref, cw_ref, cb_ref, wdn_ref, o_ref):
    i = pl.program_id(1)
    tm = x_ref.shape[1]
    n_chunks = wdn_ref.shape[0]
    g = g_ref[...]
    has_prev = (i > 0).astype(F32)
    has_next = (i < pl.num_programs(1) - 1).astype(F32)
    xe = jnp.concatenate([_rms(xp_ref[0], g) * has_prev, _rms(x_ref[0], g),
                          _rms(xn_ref[0], g) * has_next], axis=0).astype(BF16)
    acc = x_ref[0]
    for j in range(n_chunks):
        def conv(col):
            h = jnp.dot(xe, wup_ref[col], preferred_element_type=F32)
            w = cw_ref[col]
            return (h[HALO - 1:HALO - 1 + tm] * w[0:1] + h[HALO:HALO + tm] * w[1:2]
                    + h[HALO + 1:HALO + 1 + tm] * w[2:3] + cb_ref[col])
        gate = conv(j)
        up = conv(n_chunks + j)
        act = (gate * jax.nn.sigmoid(gate) * up).astype(BF16)
        acc = acc + jnp.dot(act, wdn_ref[j], preferred_element_type=F32)
    o_ref[0] = acc


def _ffn(x, g, wup, cw, cb, wdn, tm):
    b, s, d = x.shape
    nb = tm // HALO
    last = s // HALO - 1
    once = lambda arr: pl.BlockSpec(arr.shape, lambda bi, i: (0,) * arr.ndim,
                                    pipeline_mode=pl.Buffered(1))
    return pl.pallas_call(
        _ffn_kernel,
        grid=(b, s // tm),
        in_specs=[
            pl.BlockSpec((1, tm, d), lambda bi, i: (bi, i, 0)),
            pl.BlockSpec((1, HALO, d), lambda bi, i: (bi, jnp.maximum(i * nb - 1, 0), 0)),
            pl.BlockSpec((1, HALO, d), lambda bi, i: (bi, jnp.minimum((i + 1) * nb, last), 0)),
            pl.BlockSpec((1, d), lambda bi, i: (0, 0)),
            once(wup), once(cw), once(cb), once(wdn),
        ],
        out_specs=pl.BlockSpec((1, tm, d), lambda bi, i: (bi, i, 0)),
        out_shape=jax.ShapeDtypeStruct((b, s, d), F32),
        compiler_params=pltpu.CompilerParams(
            dimension_semantics=("parallel", "parallel"), vmem_limit_bytes=VMEM_LIMIT),
        name="ffn",
    )(x, x, x, g, wup, cw, cb, wdn)


def _rope_tables(seq_len):
    rows = seq_len // GRID_W
    row = jnp.repeat(jnp.arange(rows, dtype=F32), GRID_W)
    col = jnp.tile(jnp.arange(GRID_W, dtype=F32), rows)
    axis_dim = HEAD_DIM // 2
    inv_freq = 1.0 / (ROPE_THETA ** (jnp.arange(0, axis_dim, 2, dtype=F32) / axis_dim))
    ang_r = row[:, None] * inv_freq[None, :]
    ang_c = col[:, None] * inv_freq[None, :]
    cos = jnp.concatenate([jnp.cos(ang_r)] * 2 + [jnp.cos(ang_c)] * 2, axis=-1)
    sin = jnp.concatenate([jnp.sin(ang_r)] * 2 + [jnp.sin(ang_c)] * 2, axis=-1)
    first = (jnp.arange(HEAD_DIM) % 32) < 16
    sin_up = jnp.where(first, -sin, 0.0)
    sin_dn = jnp.where(first, 0.0, sin)
    tile2 = lambda t: jnp.concatenate([t, t], axis=-1)
    return tile2(cos), tile2(sin_up), tile2(sin_dn)


def _pick_tile(n, pref):
    t = min(n, pref)
    assert n % t == 0, (n, t)
    return t


def kernel(x, attn_norm_g, w_in, q_norm_g, k_norm_g, sg_norm_g, sg_w, sg_b, attn_out_g, sg_out_g,
           w_o, ffn_norm_g, w_up, conv_w, conv_b, w_down):
    b, s, d = x.shape
    depth = w_in.shape[0]
    d_ff = w_down.shape[1]
    assert s % CHUNK == 0 and s % GRID_W == 0
    tm = _pick_tile(s, 512)
    tq = _pick_tile(s, 256)
    cf = 256
    assert d_ff % cf == 0
    n_chunks = d_ff // cf

    cos, sup, sdn = _rope_tables(s)
    scale = HEAD_DIM ** -0.5
    gq = jnp.tile(q_norm_g * scale, (1, N_HEADS))[:, None, :]
    gk = jnp.tile(k_norm_g, (1, N_KV_HEADS))[:, None, :]
    gsg = sg_norm_g.reshape(depth, 1, SG_WIDTH)
    npair = SG_HEADS // 2
    wsg = sg_w.reshape(depth, npair, 2, CHUNK, CHUNK).transpose(0, 1, 3, 2, 4)
    wsg = wsg.reshape(depth, npair, CHUNK, 2 * CHUNK).astype(BF16)
    bsg = jnp.repeat(sg_b.reshape(depth, npair, 2, CHUNK).transpose(0, 1, 3, 2), HALF, axis=-1)
    w_in_b = w_in.astype(BF16)
    w_o_b = w_o.astype(BF16)
    wup = w_up.astype(BF16).reshape(depth, d, 2 * n_chunks, cf).transpose(0, 2, 1, 3)
    cw = conv_w.reshape(depth, 3, 2 * n_chunks, cf).transpose(0, 2, 1, 3)
    cb = conv_b.reshape(depth, 2 * n_chunks, 1, cf)
    wdn = w_down.astype(BF16).reshape(depth, n_chunks, cf, d)

    for i in range(depth):
        q, k2, v2, sg = _in_proj(x, attn_norm_g[i][None], w_in_b[i], gq[i], gk[i], gsg[i],
                                 cos, sup, sdn, wsg[i], bsg[i], sg_out_g[i][None], tm)
        a = _attention(q, k2, v2, attn_out_g[i][None], tq)
        x = _out_proj(x, a, sg, w_o_b[i], tm)
        x = _ffn(x, ffn_norm_g[i][None], wup[i], cw[i], cb[i], wdn[i], tm)
    return x
```

```python
import functools

import jax
import jax.numpy as jnp
import numpy as np
from jax import lax
from jax.experimental import pallas as pl
from jax.experimental.pallas import tpu as pltpu

GRID_W = 64
CHUNK = 128
N_HEADS = 8
N_KV_HEADS = 2
HEAD_DIM = 64
ATTN_WIDTH = N_HEADS * HEAD_DIM
KV_WIDTH = N_KV_HEADS * HEAD_DIM
SG_HEADS = 8
SG_WIDTH = SG_HEADS * HEAD_DIM
ROPE_THETA = 10000.0
EPS = 1e-6

LANES = 128
HALF = LANES // 2
VMEM_LIMIT = 56 * 1024 * 1024

F32 = jnp.float32
BF16 = jnp.bfloat16


def _rms(x, g):
    ms = jnp.mean(x * x, axis=-1, keepdims=True)
    return x * lax.rsqrt(ms + EPS) * g


def _low_half(shape):
    return lax.broadcasted_iota(jnp.int32, shape, len(shape) - 1) < HALF


def _head_rsqrt(xb):
    lo = _low_half(xb.shape)
    sq = xb * xb
    s_lo = jnp.sum(jnp.where(lo, sq, 0.0), axis=-1, keepdims=True)
    s_hi = jnp.sum(jnp.where(lo, 0.0, sq), axis=-1, keepdims=True)
    return lax.rsqrt(jnp.where(lo, s_lo, s_hi) * (1.0 / HEAD_DIM) + EPS)


def _gelu_tanh(x):
    c = np.float32(np.sqrt(2.0 / np.pi))
    return x * (0.5 * (1.0 + jnp.tanh(c * (x + 0.044715 * (x * x * x)))))


def _rope(xb, cos, sin_up, sin_dn):
    return xb * cos + pltpu.roll(xb, LANES - 16, 1) * sin_up + pltpu.roll(xb, 16, 1) * sin_dn


def _in_proj_kernel(x_ref, g_ref, w_ref, gq_ref, gk_ref, gsg_ref, cos_ref, sup_ref, sdn_ref,
                    wsg_ref, bsg_ref, gsgo_ref, q_ref, k_ref, v_ref, sg_ref, sgo_scr):
    tm = x_ref.shape[1]
    xn = _rms(x_ref[0], g_ref[...]).astype(BF16)
    proj = jnp.dot(xn, w_ref[...], preferred_element_type=F32)
    cos, sup, sdn = cos_ref[...], sup_ref[...], sdn_ref[...]

    for c in range(ATTN_WIDTH // LANES):
        qb = proj[:, c * LANES:(c + 1) * LANES]
        qb = qb * _head_rsqrt(qb) * gq_ref[:, c * LANES:(c + 1) * LANES]
        q_ref[0, :, c * LANES:(c + 1) * LANES] = _rope(qb, cos, sup, sdn).astype(BF16)

    kb = proj[:, ATTN_WIDTH:ATTN_WIDTH + KV_WIDTH]
    kb = _rope(kb * _head_rsqrt(kb) * gk_ref[...], cos, sup, sdn)
    k_ref[0, 0] = kb.astype(BF16)
    k_ref[0, 1] = pltpu.roll(kb, HALF, 1).astype(BF16)
    lo = _low_half((tm, LANES))
    vb = proj[:, ATTN_WIDTH + KV_WIDTH:ATTN_WIDTH + 2 * KV_WIDTH]
    vs = pltpu.roll(vb, HALF, 1)
    v_ref[0, 0] = jnp.where(lo, vb, 1.0).astype(BF16)
    v_ref[0, 1] = jnp.where(lo, 1.0, vs).astype(BF16)
    v_ref[0, 2] = jnp.where(lo, vs, 1.0).astype(BF16)
    v_ref[0, 3] = jnp.where(lo, 1.0, vb).astype(BF16)

    u0 = ATTN_WIDTH + 2 * KV_WIDTH
    for c in range(SG_WIDTH // LANES):
        ub = _gelu_tanh(proj[:, u0 + c * LANES:u0 + (c + 1) * LANES])
        vb = _gelu_tanh(proj[:, u0 + SG_WIDTH + c * LANES:u0 + SG_WIDTH + (c + 1) * LANES])
        vb = vb * _head_rsqrt(vb) * gsg_ref[:, c * LANES:(c + 1) * LANES]
        top = jnp.where(lo, vb, 0.0).astype(BF16)
        bot = jnp.where(lo, 0.0, vb).astype(BF16)
        for r in range(tm // CHUNK):
            rows = slice(r * CHUNK, (r + 1) * CHUNK)
            rhs = jnp.concatenate([top[rows], bot[rows]], axis=0)
            mixed = jnp.dot(wsg_ref[c], rhs, preferred_element_type=F32) + bsg_ref[c]
            sgo_scr[rows, c * LANES:(c + 1) * LANES] = ub[rows] * mixed
    sg_ref[0] = _rms(sgo_scr[...], gsgo_ref[...]).astype(BF16)


def _in_proj(x, g, w, gq, gk, gsg, cos, sup, sdn, wsg, bsg, gsgo, tm):
    b, s, d = x.shape
    n_in = w.shape[1]
    const = lambda *shape: pl.BlockSpec(shape, lambda bi, i: (0,) * len(shape))
    once = lambda *shape: pl.BlockSpec(shape, lambda bi, i: (0,) * len(shape),
                                       pipeline_mode=pl.Buffered(1))
    table = pl.BlockSpec((tm, LANES), lambda bi, i: (i, 0))
    return pl.pallas_call(
        _in_proj_kernel,
        grid=(b, s // tm),
        in_specs=[
            pl.BlockSpec((1, tm, d), lambda bi, i: (bi, i, 0)),
            const(1, d), once(d, n_in), const(1, ATTN_WIDTH), const(1, KV_WIDTH),
            const(1, SG_WIDTH), table, table, table,
            const(SG_WIDTH // LANES, CHUNK, 2 * CHUNK), const(SG_WIDTH // LANES, CHUNK, LANES),
            const(1, SG_WIDTH),
        ],
        out_specs=[
            pl.BlockSpec((1, tm, ATTN_WIDTH), lambda bi, i: (bi, i, 0)),
            pl.BlockSpec((1, 2, tm, KV_WIDTH), lambda bi, i: (bi, 0, i, 0)),
            pl.BlockSpec((1, 4, tm, KV_WIDTH), lambda bi, i: (bi, 0, i, 0)),
            pl.BlockSpec((1, tm, SG_WIDTH), lambda bi, i: (bi, i, 0)),
        ],
        out_shape=[
            jax.ShapeDtypeStruct((b, s, ATTN_WIDTH), BF16),
            jax.ShapeDtypeStruct((b, 2, s, KV_WIDTH), BF16),
            jax.ShapeDtypeStruct((b, 4, s, KV_WIDTH), BF16),
            jax.ShapeDtypeStruct((b, s, SG_WIDTH), BF16),
        ],
        scratch_shapes=[pltpu.VMEM((tm, SG_WIDTH), F32)],
        compiler_params=pltpu.CompilerParams(
            dimension_semantics=("parallel", "parallel"), vmem_limit_bytes=VMEM_LIMIT),
        name="in_proj",
    )(x, g, w, gq, gk, gsg, cos, sup, sdn, wsg, bsg, gsgo)


MAX_UNSHIFTED_SCORE = 40.0


def _attention_body(q_ref, k_ref, v_ref, g_ref, x_ref, sg_ref, wo_ref, o_ref, o_scr, shift):
    tq = q_ref.shape[1]
    lo = _low_half((tq, LANES))
    heads_per_kv = N_HEADS // N_KV_HEADS
    y_sg = jnp.dot(sg_ref[0], wo_ref[ATTN_WIDTH:, :], preferred_element_type=F32)

    def scores(h):
        c, half = divmod(h, 2)
        qb = q_ref[0, :, c * LANES:(c + 1) * LANES]
        keep = lo if half == 0 else jnp.logical_not(lo)
        qm = jnp.where(keep, qb, jnp.zeros_like(qb))
        kvar = 0 if h // heads_per_kv == half else 1
        return lax.dot_general(qm, k_ref[0, kvar], (((1,), (1,)), ((), ())),
                               preferred_element_type=F32)

    s = scores(0)
    halves = []
    for h in range(N_HEADS):
        s_next = scores(h + 1) if h + 1 < N_HEADS else None
        c, half = divmod(h, 2)
        if shift:
            s = s - jnp.max(s, axis=-1, keepdims=True)
        p = jnp.exp2(s).astype(BF16)
        a = jnp.dot(p, v_ref[0, 2 * (h // heads_per_kv) + half], preferred_element_type=F32)
        halves.append(a / pltpu.roll(a, HALF, 1))
        if half == 1:
            o_scr[:, c * LANES:(c + 1) * LANES] = jnp.where(lo, halves[0], halves[1])
            halves = []
        s = s_next
    a_n = _rms(o_scr[...], g_ref[...]).astype(BF16)
    y = jnp.dot(a_n, wo_ref[:ATTN_WIDTH, :], preferred_element_type=F32)
    o_ref[0] = x_ref[0] + (y + y_sg)


def _attention_kernel(bound_ref, q_ref, k_ref, v_ref, g_ref, x_ref, sg_ref, wo_ref, o_ref, o_scr):
    unshifted_ok = bound_ref[0] <= MAX_UNSHIFTED_SCORE
    args = (q_ref, k_ref, v_ref, g_ref, x_ref, sg_ref, wo_ref, o_ref, o_scr)
    pl.when(unshifted_ok)(lambda: _attention_body(*args, shift=False))
    pl.when(jnp.logical_not(unshifted_ok))(lambda: _attention_body(*args, shift=True))


def _attention(score_bound, q, k2, v4, g, x, sg, w_o, tq):
    b, s, d = x.shape
    return pl.pallas_call(
        _attention_kernel,
        grid=(b, s // tq),
        in_specs=[
            pl.BlockSpec(memory_space=pltpu.SMEM),
            pl.BlockSpec((1, tq, ATTN_WIDTH), lambda bi, i: (bi, i, 0)),
            pl.BlockSpec((1, 2, s, KV_WIDTH), lambda bi, i: (bi, 0, 0, 0)),
            pl.BlockSpec((1, 4, s, KV_WIDTH), lambda bi, i: (bi, 0, 0, 0)),
            pl.BlockSpec((1, ATTN_WIDTH), lambda bi, i: (0, 0)),
            pl.BlockSpec((1, tq, d), lambda bi, i: (bi, i, 0)),
            pl.BlockSpec((1, tq, SG_WIDTH), lambda bi, i: (bi, i, 0)),
            pl.BlockSpec(w_o.shape, lambda bi, i: (0, 0), pipeline_mode=pl.Buffered(1)),
        ],
        out_specs=pl.BlockSpec((1, tq, d), lambda bi, i: (bi, i, 0)),
        out_shape=jax.ShapeDtypeStruct((b, s, d), F32),
        scratch_shapes=[pltpu.VMEM((tq, ATTN_WIDTH), F32)],
        compiler_params=pltpu.CompilerParams(
            dimension_semantics=("parallel", "parallel"), vmem_limit_bytes=VMEM_LIMIT),
        name="attention",
    )(score_bound, q, k2, v4, g, x, sg, w_o)


SUBLANES = 8


def _ffn_kernel(x_ref, xp_ref, xn_ref, g_ref, wup_ref, cw_ref, cb_ref, wdn_ref, o_ref):
    i = pl.program_id(1)
    d = g_ref.shape[1]
    n_groups = x_ref.shape[2] // d
    tm = SUBLANES * n_groups
    n_chunks = wdn_ref.shape[0]
    g = g_ref[...]
    xt = jnp.concatenate([x_ref[0, :, k * d:(k + 1) * d] for k in range(n_groups)], axis=0)
    has_prev = (i > 0).astype(F32)
    has_next = (i < pl.num_programs(1) - 1).astype(F32)
    xe = jnp.concatenate([_rms(xp_ref[0], g) * has_prev, _rms(xt, g),
                          _rms(xn_ref[0], g) * has_next], axis=0).astype(BF16)

    def up_proj(j):
        return tuple(jnp.dot(xe, wup_ref[col], preferred_element_type=F32)
                     for col in (j, n_chunks + j))

    def conv(he, col):
        w = cw_ref[col]
        h = he[SUBLANES:SUBLANES + tm]
        sub = lax.broadcasted_iota(jnp.int32, (SUBLANES, he.shape[1]), 0)
        first_prev = jnp.where(sub == 0, pltpu.roll(he[:SUBLANES], 1, 0),
                               pltpu.roll(h[tm - SUBLANES:], 1, 0))
        last_next = jnp.where(sub == SUBLANES - 1, pltpu.roll(he[SUBLANES + tm:], SUBLANES - 1, 0),
                              pltpu.roll(h[:SUBLANES], SUBLANES - 1, 0))
        h_prev = jnp.concatenate([first_prev, h[:tm - SUBLANES]], axis=0)
        h_next = jnp.concatenate([h[SUBLANES:], last_next], axis=0)
        return h_prev * w[0:1] + h * w[1:2] + h_next * w[2:3] + cb_ref[col]

    acc = xt
    h = up_proj(0)
    for j in range(n_chunks):
        h_next = up_proj(j + 1) if j + 1 < n_chunks else None
        gate = conv(h[0], j)
        up = conv(h[1], n_chunks + j)
        act = (gate * jax.nn.sigmoid(gate) * up).astype(BF16)
        acc = acc + jnp.dot(act, wdn_ref[j], preferred_element_type=F32)
        h = h_next
    for k in range(n_groups):
        o_ref[0, :, k * d:(k + 1) * d] = acc[k * SUBLANES:(k + 1) * SUBLANES]


def _ffn(x, g, wup, cw, cb, wdn, tm):
    b, s, d = x.shape
    nt = s // tm
    n_groups = tm // SUBLANES
    xv = x.reshape(b * nt, SUBLANES, n_groups * d)
    once = lambda arr: pl.BlockSpec(arr.shape, lambda bi, i: (0,) * arr.ndim,
                                    pipeline_mode=pl.Buffered(1))
    out = pl.pallas_call(
        _ffn_kernel,
        grid=(b, nt),
        in_specs=[
            pl.BlockSpec((1, SUBLANES, n_groups * d), lambda bi, i: (bi * nt + i, 0, 0)),
            pl.BlockSpec((1, SUBLANES, d),
                         lambda bi, i: (bi * nt + jnp.maximum(i - 1, 0), 0, n_groups - 1)),
            pl.BlockSpec((1, SUBLANES, d),
                         lambda bi, i: (bi * nt + jnp.minimum(i + 1, nt - 1), 0, 0)),
            pl.BlockSpec((1, d), lambda bi, i: (0, 0)),
            once(wup), once(cw), once(cb), once(wdn),
        ],
        out_specs=pl.BlockSpec((1, SUBLANES, n_groups * d), lambda bi, i: (bi * nt + i, 0, 0)),
        out_shape=jax.ShapeDtypeStruct(xv.shape, F32),
        compiler_params=pltpu.CompilerParams(
            dimension_semantics=("parallel", "parallel"), vmem_limit_bytes=VMEM_LIMIT),
        name="ffn",
    )(xv, xv, xv, g, wup, cw, cb, wdn)
    return out.reshape(b, s, d)


def _rope_tables(seq_len):
    rows = seq_len // GRID_W
    row = jnp.repeat(jnp.arange(rows, dtype=F32), GRID_W)
    col = jnp.tile(jnp.arange(GRID_W, dtype=F32), rows)
    axis_dim = HEAD_DIM // 2
    inv_freq = 1.0 / (ROPE_THETA ** (jnp.arange(0, axis_dim, 2, dtype=F32) / axis_dim))
    ang_r = row[:, None] * inv_freq[None, :]
    ang_c = col[:, None] * inv_freq[None, :]
    cos = jnp.concatenate([jnp.cos(ang_r)] * 2 + [jnp.cos(ang_c)] * 2, axis=-1)
    sin = jnp.concatenate([jnp.sin(ang_r)] * 2 + [jnp.sin(ang_c)] * 2, axis=-1)
    first = (jnp.arange(HEAD_DIM) % 32) < 16
    sin_up = jnp.where(first, -sin, 0.0)
    sin_dn = jnp.where(first, 0.0, sin)
    tile2 = lambda t: jnp.concatenate([t, t], axis=-1)
    return tile2(cos), tile2(sin_up), tile2(sin_dn)


def _pick_tile(n, pref):
    t = min(n, pref)
    assert n % t == 0, (n, t)
    return t


def kernel(x, attn_norm_g, w_in, q_norm_g, k_norm_g, sg_norm_g, sg_w, sg_b, attn_out_g, sg_out_g,
           w_o, ffn_norm_g, w_up, conv_w, conv_b, w_down):
    b, s, d = x.shape
    depth = w_in.shape[0]
    d_ff = w_down.shape[1]
    assert s % CHUNK == 0 and s % GRID_W == 0
    tm = _pick_tile(s, 512)
    tq = _pick_tile(s, 256)
    cf = 256
    assert d_ff % cf == 0
    n_chunks = d_ff // cf

    cos, sup, sdn = _rope_tables(s)
    scale = np.float32(HEAD_DIM ** -0.5 * np.log2(np.e))
    gq = jnp.tile(q_norm_g * scale, (1, N_HEADS))[:, None, :]
    score_bound = (HEAD_DIM * scale) * (jnp.max(jnp.abs(q_norm_g), axis=-1)
                                        * jnp.max(jnp.abs(k_norm_g), axis=-1))
    gk = jnp.tile(k_norm_g, (1, N_KV_HEADS))[:, None, :]
    gsg = sg_norm_g.reshape(depth, 1, SG_WIDTH)
    npair = SG_HEADS // 2
    wsg = sg_w.reshape(depth, npair, 2, CHUNK, CHUNK).transpose(0, 1, 3, 2, 4)
    wsg = wsg.reshape(depth, npair, CHUNK, 2 * CHUNK).astype(BF16)
    bsg = jnp.repeat(sg_b.reshape(depth, npair, 2, CHUNK).transpose(0, 1, 3, 2), HALF, axis=-1)
    w_in_b = w_in.astype(BF16)
    w_o_b = w_o.astype(BF16)
    wup = w_up.astype(BF16).reshape(depth, d, 2 * n_chunks, cf).transpose(0, 2, 1, 3)
    cw = conv_w.reshape(depth, 3, 2 * n_chunks, cf).transpose(0, 2, 1, 3)
    cb = conv_b.reshape(depth, 2 * n_chunks, 1, cf)
    wdn = w_down.astype(BF16).reshape(depth, n_chunks, cf, d)

    for i in range(depth):
        q, k2, v4, sg = _in_proj(x, attn_norm_g[i][None], w_in_b[i], gq[i], gk[i], gsg[i],
                                 cos, sup, sdn, wsg[i], bsg[i], sg_out_g[i][None], tm)
        x = _attention(score_bound[i:i + 1], q, k2, v4, attn_out_g[i][None], x, sg, w_o_b[i], tq)
        x = _ffn(x, ffn_norm_g[i][None], wup[i], cw[i], cb[i], wdn[i], _pick_tile(s, 256))
    return x
```

```python
import jax
import jax.numpy as jnp
import numpy as np
from jax import lax
from jax.experimental import pallas as pl
from jax.experimental.pallas import tpu as pltpu

GRID_W = 64
CHUNK = 128
N_HEADS = 8
N_KV_HEADS = 2
HEAD_DIM = 64
ATTN_WIDTH = N_HEADS * HEAD_DIM
KV_WIDTH = N_KV_HEADS * HEAD_DIM
SG_HEADS = 8
SG_WIDTH = SG_HEADS * HEAD_DIM
ROPE_THETA = 10000.0
EPS = 1e-6

LANES = 128
SUBLANES = 8
HALF = LANES // 2
TILE = 256
GROUPS = TILE // SUBLANES
FF_CHUNK = 256
VMEM_LIMIT = 56 * 1024 * 1024

F32 = jnp.float32
BF16 = jnp.bfloat16


def _rms(x, g):
    ms = jnp.mean(x * x, axis=-1, keepdims=True)
    return x * lax.rsqrt(ms + EPS) * g


def _low_half(shape):
    return lax.broadcasted_iota(jnp.int32, shape, len(shape) - 1) < HALF


def _head_rsqrt(xb):
    lo = _low_half(xb.shape)
    sq = xb * xb
    s_lo = jnp.sum(jnp.where(lo, sq, 0.0), axis=-1, keepdims=True)
    s_hi = jnp.sum(jnp.where(lo, 0.0, sq), axis=-1, keepdims=True)
    return lax.rsqrt(jnp.where(lo, s_lo, s_hi) * (1.0 / HEAD_DIM) + EPS)


def _gelu_tanh(x):
    c = np.float32(np.sqrt(2.0 / np.pi))
    return x * (0.5 * (1.0 + jnp.tanh(c * (x + 0.044715 * (x * x * x)))))


def _rope(xb, cos, sin_up, sin_dn):
    return xb * cos + pltpu.roll(xb, LANES - 16, 1) * sin_up + pltpu.roll(xb, 16, 1) * sin_dn


def _load_tile(x_ref, d, t=0):
    return jnp.concatenate([x_ref[t, :, k * d:(k + 1) * d] for k in range(GROUPS)], axis=0)


def _store_tile(o_ref, val, d, t=0):
    for k in range(GROUPS):
        o_ref[t, :, k * d:(k + 1) * d] = val[k * SUBLANES:(k + 1) * SUBLANES]


def _tile_spec(d, nt, n=1):
    return pl.BlockSpec((n, SUBLANES, GROUPS * d), lambda bi, i: (bi * (nt // n) + i, 0, 0))


def _rows_spec(width, n=1):
    return pl.BlockSpec((1, n * TILE, width), lambda bi, i: (bi, i, 0))


def _const_spec(arr, single_buffer=False):
    mode = dict(pipeline_mode=pl.Buffered(1)) if single_buffer else {}
    return pl.BlockSpec(arr.shape, lambda bi, i: (0,) * arr.ndim, **mode)


_PARAMS = pltpu.CompilerParams(dimension_semantics=("parallel", "parallel"),
                               vmem_limit_bytes=VMEM_LIMIT)


def _in_proj_kernel(x_ref, g_ref, w_ref, gq_ref, gk_ref, gsg_ref, cos_ref, sup_ref, sdn_ref,
                    wsg_ref, bsg_ref, gsgo_ref, q_ref, k_ref, v_ref, sg_ref, sgo_scr):
    d = g_ref.shape[1]
    n_tiles = x_ref.shape[0]
    lo = _low_half((TILE, LANES))
    projs = [jnp.dot(_rms(_load_tile(x_ref, d, t), g_ref[...]).astype(BF16), w_ref[...],
                     preferred_element_type=F32) for t in range(n_tiles)]
    for t, proj in enumerate(projs):
        rows = slice(t * TILE, (t + 1) * TILE)
        cos, sup, sdn = cos_ref[rows], sup_ref[rows], sdn_ref[rows]

        for c in range(ATTN_WIDTH // LANES):
            qb = proj[:, c * LANES:(c + 1) * LANES]
            qb = qb * _head_rsqrt(qb) * gq_ref[:, c * LANES:(c + 1) * LANES]
            q_ref[0, rows, c * LANES:(c + 1) * LANES] = _rope(qb, cos, sup, sdn).astype(BF16)

        kb = proj[:, ATTN_WIDTH:ATTN_WIDTH + KV_WIDTH]
        kb = _rope(kb * _head_rsqrt(kb) * gk_ref[...], cos, sup, sdn)
        k_ref[0, 0, rows] = kb.astype(BF16)
        k_ref[0, 1, rows] = pltpu.roll(kb, HALF, 1).astype(BF16)
        vb = proj[:, ATTN_WIDTH + KV_WIDTH:ATTN_WIDTH + 2 * KV_WIDTH]
        vs = pltpu.roll(vb, HALF, 1)
        v_ref[0, 0, rows] = jnp.where(lo, vb, 1.0).astype(BF16)
        v_ref[0, 1, rows] = jnp.where(lo, 1.0, vs).astype(BF16)
        v_ref[0, 2, rows] = jnp.where(lo, vs, 1.0).astype(BF16)
        v_ref[0, 3, rows] = jnp.where(lo, 1.0, vb).astype(BF16)

        u0 = ATTN_WIDTH + 2 * KV_WIDTH
        for c in range(SG_WIDTH // LANES):
            ub = _gelu_tanh(proj[:, u0 + c * LANES:u0 + (c + 1) * LANES])
            vb = _gelu_tanh(proj[:, u0 + SG_WIDTH + c * LANES:u0 + SG_WIDTH + (c + 1) * LANES])
            vb = vb * _head_rsqrt(vb) * gsg_ref[:, c * LANES:(c + 1) * LANES]
            mixed = (jnp.dot(wsg_ref[2 * c], jnp.where(lo, vb, 0.0).astype(BF16),
                             preferred_element_type=F32)
                     + jnp.dot(wsg_ref[2 * c + 1], jnp.where(lo, 0.0, vb).astype(BF16),
                               preferred_element_type=F32))
            sgo_scr[rows, c * LANES:(c + 1) * LANES] = ub * (mixed + bsg_ref[c])
        sg_ref[0, rows] = _rms(sgo_scr[rows], gsgo_ref[...]).astype(BF16)


def _in_proj(xs, b, s, g, w, gq, gk, gsg, cos, sup, sdn, wsg, bsg, gsgo, n):
    d = g.shape[1]
    nt = s // TILE
    table = pl.BlockSpec((n * TILE, LANES), lambda bi, i: (i, 0))
    return pl.pallas_call(
        _in_proj_kernel,
        grid=(b, nt // n),
        in_specs=[
            _tile_spec(d, nt, n), _const_spec(g), _const_spec(w, True), _const_spec(gq),
            _const_spec(gk), _const_spec(gsg), table, table, table,
            _const_spec(wsg, True), _const_spec(bsg, True), _const_spec(gsgo),
        ],
        out_specs=[
            _rows_spec(ATTN_WIDTH, n),
            pl.BlockSpec((1, 2, n * TILE, KV_WIDTH), lambda bi, i: (bi, 0, i, 0)),
            pl.BlockSpec((1, 4, n * TILE, KV_WIDTH), lambda bi, i: (bi, 0, i, 0)),
            _rows_spec(SG_WIDTH, n),
        ],
        out_shape=[
            jax.ShapeDtypeStruct((b, s, ATTN_WIDTH), BF16),
            jax.ShapeDtypeStruct((b, 2, s, KV_WIDTH), BF16),
            jax.ShapeDtypeStruct((b, 4, s, KV_WIDTH), BF16),
            jax.ShapeDtypeStruct((b, s, SG_WIDTH), BF16),
        ],
        scratch_shapes=[pltpu.VMEM((n * TILE, SG_WIDTH), F32)],
        compiler_params=_PARAMS,
        name="in_proj",
    )(xs, g, w, gq, gk, gsg, cos, sup, sdn, wsg, bsg, gsgo)


MAX_UNSHIFTED_SCORE = 40.0


def _attention_body(q_ref, k_ref, v_ref, g_ref, x_ref, sg_ref, wo_ref, o_ref, o_scr, shift):
    d = x_ref.shape[2] // GROUPS
    lo = _low_half((TILE, LANES))
    heads_per_kv = N_HEADS // N_KV_HEADS
    y_sg = jnp.dot(sg_ref[0], wo_ref[ATTN_WIDTH:, :], preferred_element_type=F32)

    def scores(h):
        c, half = divmod(h, 2)
        qb = q_ref[0, :, c * LANES:(c + 1) * LANES]
        keep = lo if half == 0 else jnp.logical_not(lo)
        qm = jnp.where(keep, qb, jnp.zeros_like(qb))
        kvar = 0 if h // heads_per_kv == half else 1
        return lax.dot_general(qm, k_ref[0, kvar], (((1,), (1,)), ((), ())),
                               preferred_element_type=F32)

    s = scores(0)
    halves = []
    for h in range(N_HEADS):
        s_next = scores(h + 1) if h + 1 < N_HEADS else None
        c, half = divmod(h, 2)
        if shift:
            s = s - jnp.max(s, axis=-1, keepdims=True)
        p = jnp.exp2(s).astype(BF16)
        a = jnp.dot(p, v_ref[0, 2 * (h // heads_per_kv) + half], preferred_element_type=F32)
        halves.append(a / pltpu.roll(a, HALF, 1))
        if half == 1:
            o_scr[:, c * LANES:(c + 1) * LANES] = jnp.where(lo, halves[0], halves[1])
            halves = []
        s = s_next
    a_n = _rms(o_scr[...], g_ref[...]).astype(BF16)
    y = jnp.dot(a_n, wo_ref[:ATTN_WIDTH, :], preferred_element_type=F32)
    _store_tile(o_ref, _load_tile(x_ref, d) + (y + y_sg), d)


def _attention_kernel(bound_ref, q_ref, k_ref, v_ref, g_ref, x_ref, sg_ref, wo_ref, o_ref, o_scr):
    unshifted_ok = bound_ref[0] <= MAX_UNSHIFTED_SCORE
    args = (q_ref, k_ref, v_ref, g_ref, x_ref, sg_ref, wo_ref, o_ref, o_scr)
    pl.when(unshifted_ok)(lambda: _attention_body(*args, shift=False))
    pl.when(jnp.logical_not(unshifted_ok))(lambda: _attention_body(*args, shift=True))


def _attention(score_bound, q, k2, v4, g, xs, sg, w_o):
    b, s, _ = q.shape
    d = w_o.shape[1]
    nt = s // TILE
    return pl.pallas_call(
        _attention_kernel,
        grid=(b, nt),
        in_specs=[
            pl.BlockSpec(memory_space=pltpu.SMEM),
            _rows_spec(ATTN_WIDTH),
            pl.BlockSpec((1, 2, s, KV_WIDTH), lambda bi, i: (bi, 0, 0, 0)),
            pl.BlockSpec((1, 4, s, KV_WIDTH), lambda bi, i: (bi, 0, 0, 0)),
            _const_spec(g), _tile_spec(d, nt), _rows_spec(SG_WIDTH), _const_spec(w_o, True),
        ],
        out_specs=_tile_spec(d, nt),
        out_shape=jax.ShapeDtypeStruct(xs.shape, F32),
        scratch_shapes=[pltpu.VMEM((TILE, ATTN_WIDTH), F32)],
        compiler_params=_PARAMS,
        name="attention",
    )(score_bound, q, k2, v4, g, xs, sg, w_o)


def _ffn_kernel(x_ref, xp_ref, xn_ref, g_ref, wup_ref, cw_ref, cb_ref, wdn_ref, o_ref):
    d = g_ref.shape[1]
    n_tiles = x_ref.shape[0]
    for t in range(n_tiles):
        _ffn_tile(t, n_tiles, x_ref, xp_ref, xn_ref, g_ref, wup_ref, cw_ref, cb_ref, wdn_ref, o_ref)


def _ffn_tile(t, n_tiles, x_ref, xp_ref, xn_ref, g_ref, wup_ref, cw_ref, cb_ref, wdn_ref, o_ref):
    i = pl.program_id(1)
    d = g_ref.shape[1]
    n_chunks = wdn_ref.shape[0]
    g = g_ref[...]
    xt = _load_tile(x_ref, d, t)
    if t > 0:
        prev = _rms(x_ref[t - 1, :, (GROUPS - 1) * d:], g)
    else:
        prev = _rms(xp_ref[0], g) * (i > 0).astype(F32)
    if t + 1 < n_tiles:
        nxt = _rms(x_ref[t + 1, :, :d], g)
    else:
        nxt = _rms(xn_ref[0], g) * (i < pl.num_programs(1) - 1).astype(F32)
    xe = jnp.concatenate([prev, _rms(xt, g), nxt], axis=0).astype(BF16)

    def up_proj(j):
        return tuple(jnp.dot(xe, wup_ref[col], preferred_element_type=F32)
                     for col in (j, n_chunks + j))

    def conv(he, col):
        w = cw_ref[col]
        h = he[SUBLANES:SUBLANES + TILE]
        sub = lax.broadcasted_iota(jnp.int32, (SUBLANES, he.shape[1]), 0)
        first_prev = jnp.where(sub == 0, pltpu.roll(he[:SUBLANES], 1, 0),
                               pltpu.roll(h[TILE - SUBLANES:], 1, 0))
        last_next = jnp.where(sub == SUBLANES - 1,
                              pltpu.roll(he[SUBLANES + TILE:], SUBLANES - 1, 0),
                              pltpu.roll(h[:SUBLANES], SUBLANES - 1, 0))
        h_prev = jnp.concatenate([first_prev, h[:TILE - SUBLANES]], axis=0)
        h_next = jnp.concatenate([h[SUBLANES:], last_next], axis=0)
        return h_prev * w[0:1] + h * w[1:2] + h_next * w[2:3] + cb_ref[col]

    acc = xt
    h = up_proj(0)
    for j in range(n_chunks):
        h_next = up_proj(j + 1) if j + 1 < n_chunks else None
        gate = conv(h[0], j)
        up = conv(h[1], n_chunks + j)
        act = (gate * jax.nn.sigmoid(gate) * up).astype(BF16)
        acc = acc + jnp.dot(act, wdn_ref[j], preferred_element_type=F32)
        h = h_next
    _store_tile(o_ref, acc, d, t)


def _ffn(xs, b, s, g, wup, cw, cb, wdn, n):
    d = g.shape[1]
    nt = s // TILE
    return pl.pallas_call(
        _ffn_kernel,
        grid=(b, nt // n),
        in_specs=[
            _tile_spec(d, nt, n),
            pl.BlockSpec((1, SUBLANES, d),
                         lambda bi, i: (bi * nt + jnp.maximum(i * n - 1, 0), 0, GROUPS - 1)),
            pl.BlockSpec((1, SUBLANES, d),
                         lambda bi, i: (bi * nt + jnp.minimum((i + 1) * n, nt - 1), 0, 0)),
            _const_spec(g), _const_spec(wup, True), _const_spec(cw, True), _const_spec(cb, True),
            _const_spec(wdn, True),
        ],
        out_specs=_tile_spec(d, nt, n),
        out_shape=jax.ShapeDtypeStruct(xs.shape, F32),
        compiler_params=_PARAMS,
        name="ffn",
    )(xs, xs, xs, g, wup, cw, cb, wdn)


def _tile_order(seq_len):
    r = np.arange(seq_len)
    tile, within = r // TILE, r % TILE
    return tile * TILE + (within % SUBLANES) * GROUPS + within // SUBLANES


def _rope_tables(seq_len):
    rows = seq_len // GRID_W
    row = jnp.repeat(jnp.arange(rows, dtype=F32), GRID_W)
    col = jnp.tile(jnp.arange(GRID_W, dtype=F32), rows)
    axis_dim = HEAD_DIM // 2
    inv_freq = 1.0 / (ROPE_THETA ** (jnp.arange(0, axis_dim, 2, dtype=F32) / axis_dim))
    ang_r = row[:, None] * inv_freq[None, :]
    ang_c = col[:, None] * inv_freq[None, :]
    cos = jnp.concatenate([jnp.cos(ang_r)] * 2 + [jnp.cos(ang_c)] * 2, axis=-1)
    sin = jnp.concatenate([jnp.sin(ang_r)] * 2 + [jnp.sin(ang_c)] * 2, axis=-1)
    first = (jnp.arange(HEAD_DIM) % 32) < 16
    sin_up = jnp.where(first, -sin, 0.0)
    sin_dn = jnp.where(first, 0.0, sin)
    order = _tile_order(seq_len)
    tile2 = lambda t: jnp.concatenate([t, t], axis=-1)[order]
    return tile2(cos), tile2(sin_up), tile2(sin_dn)


def kernel(x, attn_norm_g, w_in, q_norm_g, k_norm_g, sg_norm_g, sg_w, sg_b, attn_out_g, sg_out_g,
           w_o, ffn_norm_g, w_up, conv_w, conv_b, w_down):
    b, s, d = x.shape
    depth = w_in.shape[0]
    d_ff = w_down.shape[1]
    assert s % TILE == 0 and TILE % CHUNK == 0 and s % GRID_W == 0
    assert d_ff % FF_CHUNK == 0
    n_chunks = d_ff // FF_CHUNK
    nt = s // TILE
    tiles_per_step = 2 if nt % 2 == 0 else 1

    cos, sup, sdn = _rope_tables(s)
    scale = np.float32(HEAD_DIM ** -0.5 * np.log2(np.e))
    gq = jnp.tile(q_norm_g * scale, (1, N_HEADS))[:, None, :]
    score_bound = (HEAD_DIM * scale) * (jnp.max(jnp.abs(q_norm_g), axis=-1)
                                        * jnp.max(jnp.abs(k_norm_g), axis=-1))
    gk = jnp.tile(k_norm_g, (1, N_KV_HEADS))[:, None, :]
    gsg = sg_norm_g.reshape(depth, 1, SG_WIDTH)
    pos = _tile_order(TILE)
    chunk_of, in_chunk = pos // CHUNK, pos % CHUNK
    same_chunk = jnp.asarray(chunk_of[:, None] == chunk_of[None, :])
    wsg = jnp.where(same_chunk, sg_w[:, :, in_chunk[:, None], in_chunk[None, :]], 0.0).astype(BF16)
    bsg = jnp.repeat(sg_b[:, :, in_chunk].reshape(depth, SG_HEADS // 2, 2, TILE)
                     .transpose(0, 1, 3, 2), HALF, axis=-1)
    w_in_b = w_in.astype(BF16)
    w_o_b = w_o.astype(BF16)
    wup = w_up.astype(BF16).reshape(depth, d, 2 * n_chunks, FF_CHUNK).transpose(0, 2, 1, 3)
    cw = conv_w.reshape(depth, 3, 2 * n_chunks, FF_CHUNK).transpose(0, 2, 1, 3)
    cb = conv_b.reshape(depth, 2 * n_chunks, 1, FF_CHUNK)
    wdn = w_down.astype(BF16).reshape(depth, n_chunks, FF_CHUNK, d)

    xs = x.reshape(b * nt, SUBLANES, GROUPS * d)
    for i in range(depth):
        q, k2, v4, sg = _in_proj(xs, b, s, attn_norm_g[i][None], w_in_b[i], gq[i], gk[i], gsg[i],
                                 cos, sup, sdn, wsg[i], bsg[i], sg_out_g[i][None], tiles_per_step)
        xs = _attention(score_bound[i:i + 1], q, k2, v4, attn_out_g[i][None], xs, sg, w_o_b[i])
        xs = _ffn(xs, b, s, ffn_norm_g[i][None], wup[i], cw[i], cb[i], wdn[i], tiles_per_step)
    return xs.reshape(b, s, d)
```

```python
import jax
import jax.numpy as jnp
import numpy as np
from jax import lax
from jax.experimental import pallas as pl
from jax.experimental.pallas import tpu as pltpu

GRID_W = 64
CHUNK = 128
N_HEADS = 8
N_KV_HEADS = 2
HEAD_DIM = 64
ATTN_WIDTH = N_HEADS * HEAD_DIM
KV_WIDTH = N_KV_HEADS * HEAD_DIM
SG_HEADS = 8
SG_WIDTH = SG_HEADS * HEAD_DIM
ROPE_THETA = 10000.0
EPS = 1e-6

LANES = 128
SUBLANES = 8
HALF = LANES // 2
TILE = 256
GROUPS = TILE // SUBLANES
FF_CHUNK = 256
VMEM_LIMIT = 56 * 1024 * 1024

F32 = jnp.float32
BF16 = jnp.bfloat16


def _rms(x, g):
    ms = jnp.mean(x * x, axis=-1, keepdims=True)
    return x * lax.rsqrt(ms + EPS) * g


def _low_half(shape):
    return lax.broadcasted_iota(jnp.int32, shape, len(shape) - 1) < HALF


def _head_rsqrt(xb):
    lo = _low_half(xb.shape)
    sq = xb * xb
    s_lo = jnp.sum(jnp.where(lo, sq, 0.0), axis=-1, keepdims=True)
    s_hi = jnp.sum(jnp.where(lo, 0.0, sq), axis=-1, keepdims=True)
    return lax.rsqrt(jnp.where(lo, s_lo, s_hi) * (1.0 / HEAD_DIM) + EPS)


def _gelu_tanh(x):
    c = np.float32(np.sqrt(2.0 / np.pi))
    return x * (0.5 * (1.0 + jnp.tanh(c * (x + 0.044715 * (x * x * x)))))


def _rope(xb, cos, sin_up, sin_dn):
    return xb * cos + pltpu.roll(xb, LANES - 16, 1) * sin_up + pltpu.roll(xb, 16, 1) * sin_dn


def _load_tile(x_ref, d, t=0):
    return jnp.concatenate([x_ref[t, :, k * d:(k + 1) * d] for k in range(GROUPS)], axis=0)


def _store_tile(o_ref, val, d, t=0):
    for k in range(GROUPS):
        o_ref[t, :, k * d:(k + 1) * d] = val[k * SUBLANES:(k + 1) * SUBLANES]


def _tile_spec(d, nt, n):
    return pl.BlockSpec((n, SUBLANES, GROUPS * d), lambda bi, i: (bi * (nt // n) + i, 0, 0))


def _rows_spec(width, n):
    return pl.BlockSpec((1, n * TILE, width), lambda bi, i: (bi, i, 0))


def _layer_spec(arr, layer, single_buffer=False):
    mode = dict(pipeline_mode=pl.Buffered(1)) if single_buffer else {}
    zeros = (0,) * (arr.ndim - 1)
    return pl.BlockSpec((None,) + arr.shape[1:], lambda bi, i: (layer,) + zeros, **mode)


_PARAMS = pltpu.CompilerParams(dimension_semantics=("parallel", "parallel"),
                               vmem_limit_bytes=VMEM_LIMIT)


def _in_proj_kernel(x_ref, g_ref, w_ref, gq_ref, gk_ref, gsg_ref, cos_ref, sup_ref, sdn_ref,
                    wsg_ref, bsg_ref, gsgo_ref, q_ref, k_ref, v_ref, sg_ref, sgo_scr):
    d = g_ref.shape[1]
    n_tiles = x_ref.shape[0]
    lo = _low_half((TILE, LANES))
    projs = [jnp.dot(_rms(_load_tile(x_ref, d, t), g_ref[...]).astype(BF16), w_ref[...],
                     preferred_element_type=F32) for t in range(n_tiles)]
    for t, proj in enumerate(projs):
        rows = slice(t * TILE, (t + 1) * TILE)
        cos, sup, sdn = cos_ref[rows], sup_ref[rows], sdn_ref[rows]

        for c in range(ATTN_WIDTH // LANES):
            qb = proj[:, c * LANES:(c + 1) * LANES]
            qb = qb * _head_rsqrt(qb) * gq_ref[:, c * LANES:(c + 1) * LANES]
            q_ref[0, rows, c * LANES:(c + 1) * LANES] = _rope(qb, cos, sup, sdn).astype(BF16)

        kb = proj[:, ATTN_WIDTH:ATTN_WIDTH + KV_WIDTH]
        kb = _rope(kb * _head_rsqrt(kb) * gk_ref[...], cos, sup, sdn)
        k_ref[0, 0, rows] = kb.astype(BF16)
        k_ref[0, 1, rows] = pltpu.roll(kb, HALF, 1).astype(BF16)
        vb = proj[:, ATTN_WIDTH + KV_WIDTH:ATTN_WIDTH + 2 * KV_WIDTH]
        vs = pltpu.roll(vb, HALF, 1)
        v_ref[0, 0, rows] = jnp.where(lo, vb, 1.0).astype(BF16)
        v_ref[0, 1, rows] = jnp.where(lo, 1.0, vs).astype(BF16)
        v_ref[0, 2, rows] = jnp.where(lo, vs, 1.0).astype(BF16)
        v_ref[0, 3, rows] = jnp.where(lo, 1.0, vb).astype(BF16)

        u0 = ATTN_WIDTH + 2 * KV_WIDTH
        for c in range(SG_WIDTH // LANES):
            ub = _gelu_tanh(proj[:, u0 + c * LANES:u0 + (c + 1) * LANES])
            vb = _gelu_tanh(proj[:, u0 + SG_WIDTH + c * LANES:u0 + SG_WIDTH + (c + 1) * LANES])
            vb = vb * _head_rsqrt(vb) * gsg_ref[:, c * LANES:(c + 1) * LANES]
            mixed = (jnp.dot(wsg_ref[2 * c], jnp.where(lo, vb, 0.0).astype(BF16),
                             preferred_element_type=F32)
                     + jnp.dot(wsg_ref[2 * c + 1], jnp.where(lo, 0.0, vb).astype(BF16),
                               preferred_element_type=F32))
            sgo_scr[rows, c * LANES:(c + 1) * LANES] = ub * (mixed + bsg_ref[c])
        sg_ref[0, rows] = _rms(sgo_scr[rows], gsgo_ref[...]).astype(BF16)


def _in_proj(layer, xs, b, s, g, w, gq, gk, gsg, cos, sup, sdn, wsg, bsg, gsgo, n):
    d = g.shape[-1]
    nt = s // TILE
    table = pl.BlockSpec((n * TILE, LANES), lambda bi, i: (i, 0))
    spec = lambda arr, once=False: _layer_spec(arr, layer, once)
    return pl.pallas_call(
        _in_proj_kernel,
        grid=(b, nt // n),
        in_specs=[
            _tile_spec(d, nt, n), spec(g), spec(w, True), spec(gq), spec(gk), spec(gsg),
            table, table, table, spec(wsg, True), spec(bsg, True), spec(gsgo),
        ],
        out_specs=[
            _rows_spec(ATTN_WIDTH, n),
            pl.BlockSpec((1, 2, n * TILE, KV_WIDTH), lambda bi, i: (bi, 0, i, 0)),
            pl.BlockSpec((1, 4, n * TILE, KV_WIDTH), lambda bi, i: (bi, 0, i, 0)),
            _rows_spec(SG_WIDTH, n),
        ],
        out_shape=[
            jax.ShapeDtypeStruct((b, s, ATTN_WIDTH), BF16),
            jax.ShapeDtypeStruct((b, 2, s, KV_WIDTH), BF16),
            jax.ShapeDtypeStruct((b, 4, s, KV_WIDTH), BF16),
            jax.ShapeDtypeStruct((b, s, SG_WIDTH), BF16),
        ],
        scratch_shapes=[pltpu.VMEM((n * TILE, SG_WIDTH), F32)],
        compiler_params=_PARAMS,
        name="in_proj",
    )(xs, g, w, gq, gk, gsg, cos, sup, sdn, wsg, bsg, gsgo)


MAX_UNSHIFTED_SCORE = 40.0


def _attention_body(q_ref, k_ref, v_ref, g_ref, x_ref, sg_ref, wo_ref, o_ref, o_scr, shift):
    d = x_ref.shape[2] // GROUPS
    n_tiles = x_ref.shape[0]
    lo = _low_half((TILE, LANES))
    heads_per_kv = N_HEADS // N_KV_HEADS
    tile_rows = lambda t: slice(t * TILE, (t + 1) * TILE)
    y_sg = [jnp.dot(sg_ref[0, tile_rows(t)], wo_ref[ATTN_WIDTH:, :], preferred_element_type=F32)
            for t in range(n_tiles)]

    def scores(t, h):
        c, half = divmod(h, 2)
        qb = q_ref[0, tile_rows(t), c * LANES:(c + 1) * LANES]
        keep = lo if half == 0 else jnp.logical_not(lo)
        qm = jnp.where(keep, qb, jnp.zeros_like(qb))
        kvar = 0 if h // heads_per_kv == half else 1
        return lax.dot_general(qm, k_ref[0, kvar], (((1,), (1,)), ((), ())),
                               preferred_element_type=F32)

    def finish(t):
        a_n = _rms(o_scr[tile_rows(t)], g_ref[...]).astype(BF16)
        y = jnp.dot(a_n, wo_ref[:ATTN_WIDTH, :], preferred_element_type=F32)
        _store_tile(o_ref, _load_tile(x_ref, d, t) + (y + y_sg[t]), d, t)

    items = [(t, h) for t in range(n_tiles) for h in range(N_HEADS)]
    s = scores(*items[0])
    halves = []
    for idx, (t, h) in enumerate(items):
        s_next = scores(*items[idx + 1]) if idx + 1 < len(items) else None
        c, half = divmod(h, 2)
        if shift:
            s = s - jnp.max(s, axis=-1, keepdims=True)
        p = jnp.exp2(s).astype(BF16)
        a = jnp.dot(p, v_ref[0, 2 * (h // heads_per_kv) + half], preferred_element_type=F32)
        halves.append(a / pltpu.roll(a, HALF, 1))
        if half == 1:
            o_scr[tile_rows(t), c * LANES:(c + 1) * LANES] = jnp.where(lo, halves[0], halves[1])
            halves = []
        if h == 0 and t > 0:
            finish(t - 1)
        s = s_next
    finish(n_tiles - 1)


def _attention_kernel(bound_ref, q_ref, k_ref, v_ref, g_ref, x_ref, sg_ref, wo_ref, o_ref, o_scr):
    unshifted_ok = bound_ref[0] <= MAX_UNSHIFTED_SCORE
    args = (q_ref, k_ref, v_ref, g_ref, x_ref, sg_ref, wo_ref, o_ref, o_scr)
    pl.when(unshifted_ok)(lambda: _attention_body(*args, shift=False))
    pl.when(jnp.logical_not(unshifted_ok))(lambda: _attention_body(*args, shift=True))


def _attention(layer, score_bound, q, k2, v4, g, xs, sg, w_o, n):
    b, s, _ = q.shape
    d = w_o.shape[-1]
    nt = s // TILE
    return pl.pallas_call(
        _attention_kernel,
        grid=(b, nt // n),
        in_specs=[
            pl.BlockSpec(memory_space=pltpu.SMEM),
            _rows_spec(ATTN_WIDTH, n),
            pl.BlockSpec((1, 2, s, KV_WIDTH), lambda bi, i: (bi, 0, 0, 0)),
            pl.BlockSpec((1, 4, s, KV_WIDTH), lambda bi, i: (bi, 0, 0, 0)),
            _layer_spec(g, layer), _tile_spec(d, nt, n), _rows_spec(SG_WIDTH, n),
            _layer_spec(w_o, layer, True),
        ],
        out_specs=_tile_spec(d, nt, n),
        out_shape=jax.ShapeDtypeStruct(xs.shape, F32),
        scratch_shapes=[pltpu.VMEM((n * TILE, ATTN_WIDTH), F32)],
        compiler_params=_PARAMS,
        name="attention",
    )(score_bound, q, k2, v4, g, xs, sg, w_o)


def _ffn_kernel(x_ref, xp_ref, xn_ref, g_ref, wup_ref, cw_ref, cb_ref, wdn_ref, o_ref):
    n_tiles = x_ref.shape[0]
    for t in range(n_tiles):
        _ffn_tile(t, n_tiles, x_ref, xp_ref, xn_ref, g_ref, wup_ref, cw_ref, cb_ref, wdn_ref, o_ref)


def _ffn_tile(t, n_tiles, x_ref, xp_ref, xn_ref, g_ref, wup_ref, cw_ref, cb_ref, wdn_ref, o_ref):
    i = pl.program_id(1)
    d = g_ref.shape[1]
    n_chunks, cf = wdn_ref.shape[0], wdn_ref.shape[1]
    g = g_ref[...]
    xt = _load_tile(x_ref, d, t)
    if t > 0:
        prev = _rms(x_ref[t - 1, :, (GROUPS - 1) * d:], g)
    else:
        prev = _rms(xp_ref[0], g) * (i > 0).astype(F32)
    if t + 1 < n_tiles:
        nxt = _rms(x_ref[t + 1, :, :d], g)
    else:
        nxt = _rms(xn_ref[0], g) * (i < pl.num_programs(1) - 1).astype(F32)
    xe = jnp.concatenate([prev, _rms(xt, g), nxt], axis=0).astype(BF16)
    cols = lambda j: slice(j * cf, (j + 1) * cf)

    def up_proj(j):
        return tuple(jnp.dot(xe, wup_ref[:, cols(col)], preferred_element_type=F32)
                     for col in (j, n_chunks + j))

    def conv(he, col):
        w = cw_ref[:, cols(col)]
        h = he[SUBLANES:SUBLANES + TILE]
        sub = lax.broadcasted_iota(jnp.int32, (SUBLANES, he.shape[1]), 0)
        first_prev = jnp.where(sub == 0, pltpu.roll(he[:SUBLANES], 1, 0),
                               pltpu.roll(h[TILE - SUBLANES:], 1, 0))
        last_next = jnp.where(sub == SUBLANES - 1,
                              pltpu.roll(he[SUBLANES + TILE:], SUBLANES - 1, 0),
                              pltpu.roll(h[:SUBLANES], SUBLANES - 1, 0))
        h_prev = jnp.concatenate([first_prev, h[:TILE - SUBLANES]], axis=0)
        h_next = jnp.concatenate([h[SUBLANES:], last_next], axis=0)
        return h_prev * w[0:1] + h * w[1:2] + h_next * w[2:3] + cb_ref[:, cols(col)]

    acc = xt
    h = up_proj(0)
    for j in range(n_chunks):
        h_next = up_proj(j + 1) if j + 1 < n_chunks else None
        gate = conv(h[0], j)
        up = conv(h[1], n_chunks + j)
        act = (gate * jax.nn.sigmoid(gate) * up).astype(BF16)
        acc = acc + jnp.dot(act, wdn_ref[j], preferred_element_type=F32)
        h = h_next
    _store_tile(o_ref, acc, d, t)


def _ffn(layer, xs, b, s, g, wup, cw, cb, wdn, n):
    d = g.shape[-1]
    nt = s // TILE
    spec = lambda arr, once=False: _layer_spec(arr, layer, once)
    return pl.pallas_call(
        _ffn_kernel,
        grid=(b, nt // n),
        in_specs=[
            _tile_spec(d, nt, n),
            pl.BlockSpec((1, SUBLANES, d),
                         lambda bi, i: (bi * nt + jnp.maximum(i * n - 1, 0), 0, GROUPS - 1)),
            pl.BlockSpec((1, SUBLANES, d),
                         lambda bi, i: (bi * nt + jnp.minimum((i + 1) * n, nt - 1), 0, 0)),
            spec(g), spec(wup, True), spec(cw, True), spec(cb, True), spec(wdn, True),
        ],
        out_specs=_tile_spec(d, nt, n),
        out_shape=jax.ShapeDtypeStruct(xs.shape, F32),
        compiler_params=_PARAMS,
        name="ffn",
    )(xs, xs, xs, g, wup, cw, cb, wdn)


def _to_tile_order(t):
    s = t.shape[0]
    t = t.reshape((s // TILE, SUBLANES, GROUPS) + t.shape[1:])
    return jnp.swapaxes(t, 1, 2).reshape((s,) + t.shape[3:])


def _rope_tables(seq_len):
    rows = seq_len // GRID_W
    row = jnp.repeat(jnp.arange(rows, dtype=F32), GRID_W)
    col = jnp.tile(jnp.arange(GRID_W, dtype=F32), rows)
    axis_dim = HEAD_DIM // 2
    inv_freq = 1.0 / (ROPE_THETA ** (jnp.arange(0, axis_dim, 2, dtype=F32) / axis_dim))
    ang_r = row[:, None] * inv_freq[None, :]
    ang_c = col[:, None] * inv_freq[None, :]
    cos = jnp.concatenate([jnp.cos(ang_r)] * 2 + [jnp.cos(ang_c)] * 2, axis=-1)
    sin = jnp.concatenate([jnp.sin(ang_r)] * 2 + [jnp.sin(ang_c)] * 2, axis=-1)
    first = (jnp.arange(HEAD_DIM) % 32) < 16
    sin_up = jnp.where(first, -sin, 0.0)
    sin_dn = jnp.where(first, 0.0, sin)
    tile2 = lambda t: _to_tile_order(jnp.concatenate([t, t], axis=-1))
    return tile2(cos), tile2(sin_up), tile2(sin_dn)


def _gating_params(sg_w, sg_b):
    depth = sg_w.shape[0]
    per_chunk = CHUNK // GROUPS
    n_chunk = TILE // CHUNK
    assert per_chunk * GROUPS == CHUNK and per_chunk * n_chunk == SUBLANES
    w = sg_w.reshape(depth, SG_HEADS, per_chunk, GROUPS, per_chunk, GROUPS)
    w = w.transpose(0, 1, 3, 2, 5, 4)
    eye = jnp.eye(n_chunk, dtype=sg_w.dtype)
    w = w[:, :, :, None, :, :, None, :] * eye[None, None, None, :, None, None, :, None]
    wsg = w.reshape(depth, SG_HEADS, TILE, TILE).astype(BF16)
    bias = sg_b.reshape(depth, SG_HEADS, per_chunk, GROUPS).transpose(0, 1, 3, 2)
    bias = jnp.broadcast_to(bias[:, :, :, None, :],
                            (depth, SG_HEADS, GROUPS, n_chunk, per_chunk)).reshape(
                                depth, SG_HEADS // 2, 2, TILE)
    bsg = jnp.repeat(bias.transpose(0, 1, 3, 2), HALF, axis=-1)
    return wsg, bsg


def kernel(x, attn_norm_g, w_in, q_norm_g, k_norm_g, sg_norm_g, sg_w, sg_b, attn_out_g, sg_out_g,
           w_o, ffn_norm_g, w_up, conv_w, conv_b, w_down):
    b, s, d = x.shape
    depth = w_in.shape[0]
    d_ff = w_down.shape[1]
    assert s % TILE == 0 and s % GRID_W == 0 and d_ff % FF_CHUNK == 0
    nt = s // TILE
    n = 2 if nt % 2 == 0 else 1

    cos, sup, sdn = _rope_tables(s)
    scale = np.float32(HEAD_DIM ** -0.5 * np.log2(np.e))
    gq = jnp.tile(q_norm_g * scale, (1, N_HEADS))[:, None, :]
    score_bound = (HEAD_DIM * scale) * (jnp.max(jnp.abs(q_norm_g), axis=-1)
                                        * jnp.max(jnp.abs(k_norm_g), axis=-1))
    gk = jnp.tile(k_norm_g, (1, N_KV_HEADS))[:, None, :]
    gsg = sg_norm_g.reshape(depth, 1, SG_WIDTH)
    wsg, bsg = _gating_params(sg_w, sg_b)
    row = lambda p: p[:, None, :]
    w_in_b = w_in.astype(BF16)
    w_o_b = w_o.astype(BF16)
    wup = w_up.astype(BF16)
    wdn = w_down.astype(BF16).reshape(depth, d_ff // FF_CHUNK, FF_CHUNK, d)

    xs = x.reshape(b * nt, SUBLANES, GROUPS * d)
    for i in range(depth):
        q, k2, v4, sg = _in_proj(i, xs, b, s, row(attn_norm_g), w_in_b, gq, gk, gsg,
                                 cos, sup, sdn, wsg, bsg, row(sg_out_g), n)
        xs = _attention(i, score_bound[i:i + 1], q, k2, v4, row(attn_out_g), xs, sg, w_o_b, n)
        xs = _ffn(i, xs, b, s, row(ffn_norm_g), wup, conv_w, row(conv_b), wdn, n)
    return xs.reshape(b, s, d)
```

```python
import jax
import jax.numpy as jnp
import numpy as np
from jax import lax
from jax.experimental import pallas as pl
from jax.experimental.pallas import tpu as pltpu

GRID_W = 64
CHUNK = 128
N_HEADS = 8
N_KV_HEADS = 2
HEAD_DIM = 64
ATTN_WIDTH = N_HEADS * HEAD_DIM
KV_WIDTH = N_KV_HEADS * HEAD_DIM
SG_HEADS = 8
SG_WIDTH = SG_HEADS * HEAD_DIM
ROPE_THETA = 10000.0
EPS = 1e-6

LANES = 128
SUBLANES = 8
HALF = LANES // 2
TILE = 256
GROUPS = TILE // SUBLANES
FF_CHUNK = 256
VMEM_LIMIT = 56 * 1024 * 1024

F32 = jnp.float32
BF16 = jnp.bfloat16


def _rms(x, g):
    ms = jnp.mean(x * x, axis=-1, keepdims=True)
    return x * lax.rsqrt(ms + EPS) * g


def _low_half(shape):
    return lax.broadcasted_iota(jnp.int32, shape, len(shape) - 1) < HALF


def _head_rsqrt(xb):
    lo = _low_half(xb.shape)
    sq = xb * xb
    s_lo = jnp.sum(jnp.where(lo, sq, 0.0), axis=-1, keepdims=True)
    s_hi = jnp.sum(jnp.where(lo, 0.0, sq), axis=-1, keepdims=True)
    return lax.rsqrt(jnp.where(lo, s_lo, s_hi) * (1.0 / HEAD_DIM) + EPS)


def _gelu_tanh(x):
    c = np.float32(np.sqrt(2.0 / np.pi))
    return x * (0.5 * (1.0 + jnp.tanh(c * (x + 0.044715 * (x * x * x)))))


def _rope(xb, cos, sin_up, sin_dn):
    return xb * cos + pltpu.roll(xb, LANES - 16, 1) * sin_up + pltpu.roll(xb, 16, 1) * sin_dn


def _load_tile(x_ref, d, t=0):
    return jnp.concatenate([x_ref[t, :, k * d:(k + 1) * d] for k in range(GROUPS)], axis=0)


def _store_tile(o_ref, val, d, t=0):
    for k in range(GROUPS):
        o_ref[t, :, k * d:(k + 1) * d] = val[k * SUBLANES:(k + 1) * SUBLANES]


def _tile_spec(d, nt, n):
    return pl.BlockSpec((n, SUBLANES, GROUPS * d), lambda bi, i: (bi * (nt // n) + i, 0, 0))


def _rows_spec(width, n):
    return pl.BlockSpec((1, n * TILE, width), lambda bi, i: (bi, i, 0))


def _layer_spec(arr, layer, single_buffer=False):
    mode = dict(pipeline_mode=pl.Buffered(1)) if single_buffer else {}
    zeros = (0,) * (arr.ndim - 1)
    return pl.BlockSpec((None,) + arr.shape[1:], lambda bi, i: (layer,) + zeros, **mode)


_PARAMS = pltpu.CompilerParams(dimension_semantics=("parallel", "parallel"),
                               vmem_limit_bytes=VMEM_LIMIT)


def _in_proj_kernel(x_ref, g_ref, w_ref, gq_ref, gk_ref, gsg_ref, cos_ref, sup_ref, sdn_ref,
                    wsg_ref, bsg_ref, gsgo_ref, q_ref, k_ref, v_ref, sg_ref, sgo_scr):
    d = g_ref.shape[1]
    n_tiles = x_ref.shape[0]
    lo = _low_half((TILE, LANES))
    projs = [jnp.dot(_rms(_load_tile(x_ref, d, t), g_ref[...]).astype(BF16), w_ref[...],
                     preferred_element_type=F32) for t in range(n_tiles)]
    for t, proj in enumerate(projs):
        rows = slice(t * TILE, (t + 1) * TILE)
        cos, sup, sdn = cos_ref[rows], sup_ref[rows], sdn_ref[rows]

        for c in range(ATTN_WIDTH // LANES):
            qb = proj[:, c * LANES:(c + 1) * LANES]
            qb = qb * _head_rsqrt(qb) * gq_ref[:, c * LANES:(c + 1) * LANES]
            q_ref[0, rows, c * LANES:(c + 1) * LANES] = _rope(qb, cos, sup, sdn).astype(BF16)

        kb = proj[:, ATTN_WIDTH:ATTN_WIDTH + KV_WIDTH]
        kb = _rope(kb * _head_rsqrt(kb) * gk_ref[...], cos, sup, sdn)
        k_ref[0, 0, rows] = kb.astype(BF16)
        k_ref[0, 1, rows] = pltpu.roll(kb, HALF, 1).astype(BF16)
        vb = proj[:, ATTN_WIDTH + KV_WIDTH:ATTN_WIDTH + 2 * KV_WIDTH]
        vs = pltpu.roll(vb, HALF, 1)
        v_ref[0, 0, rows] = jnp.where(lo, vb, 1.0).astype(BF16)
        v_ref[0, 1, rows] = jnp.where(lo, 1.0, vs).astype(BF16)
        v_ref[0, 2, rows] = jnp.where(lo, vs, 1.0).astype(BF16)
        v_ref[0, 3, rows] = jnp.where(lo, 1.0, vb).astype(BF16)

        u0 = ATTN_WIDTH + 2 * KV_WIDTH
        for c in range(SG_WIDTH // LANES):
            ub = _gelu_tanh(proj[:, u0 + c * LANES:u0 + (c + 1) * LANES])
            vb = _gelu_tanh(proj[:, u0 + SG_WIDTH + c * LANES:u0 + SG_WIDTH + (c + 1) * LANES])
            vb = vb * _head_rsqrt(vb) * gsg_ref[:, c * LANES:(c + 1) * LANES]
            mixed = (jnp.dot(wsg_ref[2 * c], jnp.where(lo, vb, 0.0).astype(BF16),
                             preferred_element_type=F32)
                     + jnp.dot(wsg_ref[2 * c + 1], jnp.where(lo, 0.0, vb).astype(BF16),
                               preferred_element_type=F32))
            sgo_scr[rows, c * LANES:(c + 1) * LANES] = ub * (mixed + bsg_ref[c])
        sg_ref[0, rows] = _rms(sgo_scr[rows], gsgo_ref[...]).astype(BF16)


def _in_proj(layer, xs, b, s, g, w, gq, gk, gsg, cos, sup, sdn, wsg, bsg, gsgo, n):
    d = g.shape[-1]
    nt = s // TILE
    table = pl.BlockSpec((n * TILE, LANES), lambda bi, i: (i, 0))
    spec = lambda arr, once=False: _layer_spec(arr, layer, once)
    return pl.pallas_call(
        _in_proj_kernel,
        grid=(b, nt // n),
        in_specs=[
            _tile_spec(d, nt, n), spec(g), spec(w, True), spec(gq), spec(gk), spec(gsg),
            table, table, table, spec(wsg, True), spec(bsg, True), spec(gsgo),
        ],
        out_specs=[
            _rows_spec(ATTN_WIDTH, n),
            pl.BlockSpec((1, 2, n * TILE, KV_WIDTH), lambda bi, i: (bi, 0, i, 0)),
            pl.BlockSpec((1, 4, n * TILE, KV_WIDTH), lambda bi, i: (bi, 0, i, 0)),
            _rows_spec(SG_WIDTH, n),
        ],
        out_shape=[
            jax.ShapeDtypeStruct((b, s, ATTN_WIDTH), BF16),
            jax.ShapeDtypeStruct((b, 2, s, KV_WIDTH), BF16),
            jax.ShapeDtypeStruct((b, 4, s, KV_WIDTH), BF16),
            jax.ShapeDtypeStruct((b, s, SG_WIDTH), BF16),
        ],
        scratch_shapes=[pltpu.VMEM((n * TILE, SG_WIDTH), F32)],
        compiler_params=_PARAMS,
        name="in_proj",
    )(xs, g, w, gq, gk, gsg, cos, sup, sdn, wsg, bsg, gsgo)


MAX_UNSHIFTED_SCORE = 40.0


def _attention_body(q_ref, k_ref, v_ref, g_ref, x_ref, sg_ref, wo_ref, o_ref, o_scr, shift):
    d = x_ref.shape[2] // GROUPS
    n_tiles = x_ref.shape[0]
    lo = _low_half((TILE, LANES))
    heads_per_kv = N_HEADS // N_KV_HEADS
    tile_rows = lambda t: slice(t * TILE, (t + 1) * TILE)
    y_sg = [jnp.dot(sg_ref[0, tile_rows(t)], wo_ref[ATTN_WIDTH:, :], preferred_element_type=F32)
            for t in range(n_tiles)]

    def scores(t, h):
        c, half = divmod(h, 2)
        qb = q_ref[0, tile_rows(t), c * LANES:(c + 1) * LANES]
        keep = lo if half == 0 else jnp.logical_not(lo)
        qm = jnp.where(keep, qb, jnp.zeros_like(qb))
        kvar = 0 if h // heads_per_kv == half else 1
        return lax.dot_general(qm, k_ref[0, kvar], (((1,), (1,)), ((), ())),
                               preferred_element_type=F32)

    def finish(t):
        a_n = _rms(o_scr[tile_rows(t)], g_ref[...]).astype(BF16)
        y = jnp.dot(a_n, wo_ref[:ATTN_WIDTH, :], preferred_element_type=F32)
        _store_tile(o_ref, _load_tile(x_ref, d, t) + (y + y_sg[t]), d, t)

    items = [(t, h) for t in range(n_tiles) for h in range(N_HEADS)]
    s = scores(*items[0])
    halves = []
    for idx, (t, h) in enumerate(items):
        s_next = scores(*items[idx + 1]) if idx + 1 < len(items) else None
        c, half = divmod(h, 2)
        if shift:
            s = s - jnp.max(s, axis=-1, keepdims=True)
        p = jnp.exp2(s).astype(BF16)
        a = jnp.dot(p, v_ref[0, 2 * (h // heads_per_kv) + half], preferred_element_type=F32)
        halves.append(a / pltpu.roll(a, HALF, 1))
        if half == 1:
            o_scr[tile_rows(t), c * LANES:(c + 1) * LANES] = jnp.where(lo, halves[0], halves[1])
            halves = []
        if h == 0 and t > 0:
            finish(t - 1)
        s = s_next
    finish(n_tiles - 1)


def _attention_kernel(bound_ref, q_ref, k_ref, v_ref, g_ref, x_ref, sg_ref, wo_ref, o_ref, o_scr):
    unshifted_ok = bound_ref[0] <= MAX_UNSHIFTED_SCORE
    args = (q_ref, k_ref, v_ref, g_ref, x_ref, sg_ref, wo_ref, o_ref, o_scr)
    pl.when(unshifted_ok)(lambda: _attention_body(*args, shift=False))
    pl.when(jnp.logical_not(unshifted_ok))(lambda: _attention_body(*args, shift=True))


def _attention(layer, score_bound, q, k2, v4, g, xs, sg, w_o, n):
    b, s, _ = q.shape
    d = w_o.shape[-1]
    nt = s // TILE
    return pl.pallas_call(
        _attention_kernel,
        grid=(b, nt // n),
        in_specs=[
            pl.BlockSpec(memory_space=pltpu.SMEM),
            _rows_spec(ATTN_WIDTH, n),
            pl.BlockSpec((1, 2, s, KV_WIDTH), lambda bi, i: (bi, 0, 0, 0)),
            pl.BlockSpec((1, 4, s, KV_WIDTH), lambda bi, i: (bi, 0, 0, 0)),
            _layer_spec(g, layer), _tile_spec(d, nt, n), _rows_spec(SG_WIDTH, n),
            _layer_spec(w_o, layer, True),
        ],
        out_specs=_tile_spec(d, nt, n),
        out_shape=jax.ShapeDtypeStruct(xs.shape, F32),
        scratch_shapes=[pltpu.VMEM((n * TILE, ATTN_WIDTH), F32)],
        compiler_params=_PARAMS,
        name="attention",
    )(score_bound, q, k2, v4, g, xs, sg, w_o)


def _ffn_kernel(x_ref, xp_ref, xn_ref, g_ref, wup_ref, cw_ref, cb_ref, wdn_ref, o_ref):
    n_tiles = x_ref.shape[0]
    for t in range(n_tiles):
        _ffn_tile(t, n_tiles, x_ref, xp_ref, xn_ref, g_ref, wup_ref, cw_ref, cb_ref, wdn_ref, o_ref)


def _ffn_tile(t, n_tiles, x_ref, xp_ref, xn_ref, g_ref, wup_ref, cw_ref, cb_ref, wdn_ref, o_ref):
    i = pl.program_id(1)
    d = g_ref.shape[1]
    n_chunks, cf = wdn_ref.shape[0], wdn_ref.shape[1]
    g = g_ref[...]
    xt = _load_tile(x_ref, d, t)
    if t > 0:
        prev = _rms(x_ref[t - 1, :, (GROUPS - 1) * d:], g)
    else:
        prev = _rms(xp_ref[0], g) * (i > 0).astype(F32)
    if t + 1 < n_tiles:
        nxt = _rms(x_ref[t + 1, :, :d], g)
    else:
        nxt = _rms(xn_ref[0], g) * (i < pl.num_programs(1) - 1).astype(F32)
    xe = jnp.concatenate([prev, _rms(xt, g), nxt], axis=0).astype(BF16)
    cols = lambda j: slice(j * cf, (j + 1) * cf)

    def up_proj(j):
        return tuple(jnp.dot(xe, wup_ref[:, cols(col)], preferred_element_type=F32)
                     for col in (j, n_chunks + j))

    def conv(he, col):
        w = cw_ref[:, cols(col)]
        h = he[SUBLANES:SUBLANES + TILE]
        sub = lax.broadcasted_iota(jnp.int32, (SUBLANES, he.shape[1]), 0)
        first_prev = jnp.where(sub == 0, pltpu.roll(he[:SUBLANES], 1, 0),
                               pltpu.roll(h[TILE - SUBLANES:], 1, 0))
        last_next = jnp.where(sub == SUBLANES - 1,
                              pltpu.roll(he[SUBLANES + TILE:], SUBLANES - 1, 0),
                              pltpu.roll(h[:SUBLANES], SUBLANES - 1, 0))
        h_prev = jnp.concatenate([first_prev, h[:TILE - SUBLANES]], axis=0)
        h_next = jnp.concatenate([h[SUBLANES:], last_next], axis=0)
        return h_prev * w[0:1] + h * w[1:2] + h_next * w[2:3] + cb_ref[:, cols(col)]

    acc = xt
    h = up_proj(0)
    for j in range(n_chunks):
        h_next = up_proj(j + 1) if j + 1 < n_chunks else None
        gate = conv(h[0], j)
        up = conv(h[1], n_chunks + j)
        act = (gate * jax.nn.sigmoid(gate) * up).astype(BF16)
        acc = acc + jnp.dot(act, wdn_ref[j], preferred_element_type=F32)
        h = h_next
    _store_tile(o_ref, acc, d, t)


def _ffn(layer, xs, b, s, g, wup, cw, cb, wdn, n):
    d = g.shape[-1]
    nt = s // TILE
    spec = lambda arr, once=False: _layer_spec(arr, layer, once)
    return pl.pallas_call(
        _ffn_kernel,
        grid=(b, nt // n),
        in_specs=[
            _tile_spec(d, nt, n),
            pl.BlockSpec((1, SUBLANES, d),
                         lambda bi, i: (bi * nt + jnp.maximum(i * n - 1, 0), 0, GROUPS - 1)),
            pl.BlockSpec((1, SUBLANES, d),
                         lambda bi, i: (bi * nt + jnp.minimum((i + 1) * n, nt - 1), 0, 0)),
            spec(g), spec(wup, True), spec(cw, True), spec(cb, True), spec(wdn, True),
        ],
        out_specs=_tile_spec(d, nt, n),
        out_shape=jax.ShapeDtypeStruct(xs.shape, F32),
        compiler_params=_PARAMS,
        name="ffn",
    )(xs, xs, xs, g, wup, cw, cb, wdn)


def _to_tile_order(t):
    s = t.shape[0]
    t = t.reshape((s // TILE, SUBLANES, GROUPS) + t.shape[1:])
    return jnp.swapaxes(t, 1, 2).reshape((s,) + t.shape[3:])


def _rope_tables(seq_len):
    rows = seq_len // GRID_W
    row = jnp.repeat(jnp.arange(rows, dtype=F32), GRID_W)
    col = jnp.tile(jnp.arange(GRID_W, dtype=F32), rows)
    axis_dim = HEAD_DIM // 2
    inv_freq = 1.0 / (ROPE_THETA ** (jnp.arange(0, axis_dim, 2, dtype=F32) / axis_dim))
    ang_r = row[:, None] * inv_freq[None, :]
    ang_c = col[:, None] * inv_freq[None, :]
    cos = jnp.concatenate([jnp.cos(ang_r)] * 2 + [jnp.cos(ang_c)] * 2, axis=-1)
    sin = jnp.concatenate([jnp.sin(ang_r)] * 2 + [jnp.sin(ang_c)] * 2, axis=-1)
    first = (jnp.arange(HEAD_DIM) % 32) < 16
    sin_up = jnp.where(first, -sin, 0.0)
    sin_dn = jnp.where(first, 0.0, sin)
    tile2 = lambda t: _to_tile_order(jnp.concatenate([t, t], axis=-1))
    return tile2(cos), tile2(sin_up), tile2(sin_dn)


def _gating_params(sg_w, sg_b):
    depth = sg_w.shape[0]
    per_chunk = CHUNK // GROUPS
    n_chunk = TILE // CHUNK
    assert per_chunk * GROUPS == CHUNK and per_chunk * n_chunk == SUBLANES
    w = sg_w.reshape(depth, SG_HEADS, per_chunk, GROUPS, per_chunk, GROUPS)
    w = w.transpose(0, 1, 3, 2, 5, 4)
    eye = jnp.eye(n_chunk, dtype=sg_w.dtype)
    w = w[:, :, :, None, :, :, None, :] * eye[None, None, None, :, None, None, :, None]
    wsg = w.reshape(depth, SG_HEADS, TILE, TILE).astype(BF16)
    bias = sg_b.reshape(depth, SG_HEADS, per_chunk, GROUPS).transpose(0, 1, 3, 2)
    bias = jnp.broadcast_to(bias[:, :, :, None, :],
                            (depth, SG_HEADS, GROUPS, n_chunk, per_chunk)).reshape(
                                depth, SG_HEADS // 2, 2, TILE)
    bsg = jnp.repeat(bias.transpose(0, 1, 3, 2), HALF, axis=-1)
    return wsg, bsg


def kernel(x, attn_norm_g, w_in, q_norm_g, k_norm_g, sg_norm_g, sg_w, sg_b, attn_out_g, sg_out_g,
           w_o, ffn_norm_g, w_up, conv_w, conv_b, w_down):
    b, s, d = x.shape
    depth = w_in.shape[0]
    d_ff = w_down.shape[1]
    assert s % TILE == 0 and s % GRID_W == 0 and d_ff % FF_CHUNK == 0
    nt = s // TILE
    n = 2 if nt % 2 == 0 else 1

    cos, sup, sdn = _rope_tables(s)
    scale = np.float32(HEAD_DIM ** -0.5 * np.log2(np.e))
    gq = jnp.tile(q_norm_g * scale, (1, N_HEADS))[:, None, :]
    score_bound = (HEAD_DIM * scale) * (jnp.max(jnp.abs(q_norm_g), axis=-1)
                                        * jnp.max(jnp.abs(k_norm_g), axis=-1))
    gk = jnp.tile(k_norm_g, (1, N_KV_HEADS))[:, None, :]
    gsg = sg_norm_g.reshape(depth, 1, SG_WIDTH)
    wsg, bsg = _gating_params(sg_w, sg_b)
    row = lambda p: p[:, None, :]
    w_in_b = w_in.astype(BF16)
    w_o_b = w_o.astype(BF16)
    wup = w_up.astype(BF16)
    wdn = w_down.astype(BF16).reshape(depth, d_ff // FF_CHUNK, FF_CHUNK, d)

    xs = x.reshape(b * nt, SUBLANES, GROUPS * d)
    for i in range(depth):
        q, k2, v4, sg = _in_proj(i, xs, b, s, row(attn_norm_g), w_in_b, gq, gk, gsg,
                                 cos, sup, sdn, wsg, bsg, row(sg_out_g), n)
        xs = _attention(i, score_bound[i:i + 1], q, k2, v4, row(attn_out_g), xs, sg, w_o_b, 1)
        xs = _ffn(i, xs, b, s, row(ffn_norm_g), wup, conv_w, row(conv_b), wdn, n)
    return xs.reshape(b, s, d)
```

```python
import jax
import jax.numpy as jnp
import numpy as np
from jax import lax
from jax.experimental import pallas as pl
from jax.experimental.pallas import tpu as pltpu

GRID_W = 64
CHUNK = 128
N_HEADS = 8
N_KV_HEADS = 2
HEAD_DIM = 64
ATTN_WIDTH = N_HEADS * HEAD_DIM
KV_WIDTH = N_KV_HEADS * HEAD_DIM
SG_HEADS = 8
SG_WIDTH = SG_HEADS * HEAD_DIM
ROPE_THETA = 10000.0
EPS = 1e-6

LANES = 128
SUBLANES = 8
HALF = LANES // 2
TILE = 256
GROUPS = TILE // SUBLANES
FF_CHUNK = 256
VMEM_LIMIT = 56 * 1024 * 1024

F32 = jnp.float32
BF16 = jnp.bfloat16


def _rms(x, g):
    ms = jnp.mean(x * x, axis=-1, keepdims=True)
    return x * lax.rsqrt(ms + EPS) * g


def _low_half(shape):
    return lax.broadcasted_iota(jnp.int32, shape, len(shape) - 1) < HALF


def _head_rsqrt(xb):
    lo = _low_half(xb.shape)
    sq = xb * xb
    s_lo = jnp.sum(jnp.where(lo, sq, 0.0), axis=-1, keepdims=True)
    s_hi = jnp.sum(jnp.where(lo, 0.0, sq), axis=-1, keepdims=True)
    return lax.rsqrt(jnp.where(lo, s_lo, s_hi) * (1.0 / HEAD_DIM) + EPS)


def _gelu_tanh(x):
    c = np.float32(np.sqrt(2.0 / np.pi))
    return x * (0.5 * (1.0 + jnp.tanh(c * (x + 0.044715 * (x * x * x)))))


def _rope(xb, cos, sin_up, sin_dn):
    return xb * cos + pltpu.roll(xb, LANES - 16, 1) * sin_up + pltpu.roll(xb, 16, 1) * sin_dn


def _load_tile(x_ref, d, t=0):
    return jnp.concatenate([x_ref[t, :, k * d:(k + 1) * d] for k in range(GROUPS)], axis=0)


def _store_tile(o_ref, val, d, t=0):
    for k in range(GROUPS):
        o_ref[t, :, k * d:(k + 1) * d] = val[k * SUBLANES:(k + 1) * SUBLANES]


def _tile_spec(d, nt, n):
    return pl.BlockSpec((n, SUBLANES, GROUPS * d), lambda bi, i: (bi * (nt // n) + i, 0, 0))


def _rows_spec(width, n):
    return pl.BlockSpec((1, n * TILE, width), lambda bi, i: (bi, i, 0))


def _layer_spec(arr, layer, single_buffer=False):
    mode = dict(pipeline_mode=pl.Buffered(1)) if single_buffer else {}
    zeros = (0,) * (arr.ndim - 1)
    return pl.BlockSpec((None,) + arr.shape[1:], lambda bi, i: (layer,) + zeros, **mode)


_PARAMS = pltpu.CompilerParams(dimension_semantics=("parallel", "parallel"),
                               vmem_limit_bytes=VMEM_LIMIT)


def _strided_rows(t, k):
    return pl.ds(t * TILE + k, SUBLANES, stride=GROUPS)


def _load_natural_tile(x_ref, xs_ref, slab, d, t):
    pieces = []
    for k in range(GROUPS):
        piece = jnp.concatenate([slab[j, _strided_rows(t, k), :] for j in range(d // LANES)], axis=1)
        xs_ref[t, :, k * d:(k + 1) * d] = piece
        pieces.append(piece)
    return jnp.concatenate(pieces, axis=0)


def _in_proj_kernel(x_ref, g_ref, w_ref, gq_ref, gk_ref, gsg_ref, cos_ref, sup_ref, sdn_ref,
                    wsg_ref, bsg_ref, gsgo_ref, q_ref, k_ref, v_ref, sg_ref, *rest):
    d = g_ref.shape[1]
    lo = _low_half((TILE, LANES))
    if len(rest) == 1:
        sgo_scr, = rest
        n_tiles = x_ref.shape[0]
        tiles = [_load_tile(x_ref, d, t) for t in range(n_tiles)]
    else:
        xs_ref, sgo_scr, slab = rest
        n_tiles = x_ref.shape[1] // TILE
        for j in range(d // LANES):
            slab[j] = x_ref[0, :, j * LANES:(j + 1) * LANES]
        tiles = [_load_natural_tile(x_ref, xs_ref, slab, d, t) for t in range(n_tiles)]
    projs = [jnp.dot(_rms(xt, g_ref[...]).astype(BF16), w_ref[...], preferred_element_type=F32)
             for xt in tiles]
    for t, proj in enumerate(projs):
        rows = slice(t * TILE, (t + 1) * TILE)
        cos, sup, sdn = cos_ref[rows], sup_ref[rows], sdn_ref[rows]

        for c in range(ATTN_WIDTH // LANES):
            qb = proj[:, c * LANES:(c + 1) * LANES]
            qb = qb * _head_rsqrt(qb) * gq_ref[:, c * LANES:(c + 1) * LANES]
            q_ref[0, rows, c * LANES:(c + 1) * LANES] = _rope(qb, cos, sup, sdn).astype(BF16)

        kb = proj[:, ATTN_WIDTH:ATTN_WIDTH + KV_WIDTH]
        kb = _rope(kb * _head_rsqrt(kb) * gk_ref[...], cos, sup, sdn)
        k_ref[0, 0, rows] = kb.astype(BF16)
        k_ref[0, 1, rows] = pltpu.roll(kb, HALF, 1).astype(BF16)
        vb = proj[:, ATTN_WIDTH + KV_WIDTH:ATTN_WIDTH + 2 * KV_WIDTH]
        vs = pltpu.roll(vb, HALF, 1)
        v_ref[0, 0, rows] = jnp.where(lo, vb, 1.0).astype(BF16)
        v_ref[0, 1, rows] = jnp.where(lo, 1.0, vs).astype(BF16)
        v_ref[0, 2, rows] = jnp.where(lo, vs, 1.0).astype(BF16)
        v_ref[0, 3, rows] = jnp.where(lo, 1.0, vb).astype(BF16)

        u0 = ATTN_WIDTH + 2 * KV_WIDTH
        for c in range(SG_WIDTH // LANES):
            ub = _gelu_tanh(proj[:, u0 + c * LANES:u0 + (c + 1) * LANES])
            vb = _gelu_tanh(proj[:, u0 + SG_WIDTH + c * LANES:u0 + SG_WIDTH + (c + 1) * LANES])
            vb = vb * _head_rsqrt(vb) * gsg_ref[:, c * LANES:(c + 1) * LANES]
            mixed = (jnp.dot(wsg_ref[2 * c], jnp.where(lo, vb, 0.0).astype(BF16),
                             preferred_element_type=F32)
                     + jnp.dot(wsg_ref[2 * c + 1], jnp.where(lo, 0.0, vb).astype(BF16),
                               preferred_element_type=F32))
            sgo_scr[rows, c * LANES:(c + 1) * LANES] = ub * (mixed + bsg_ref[c])
        sg_ref[0, rows] = _rms(sgo_scr[rows], gsgo_ref[...]).astype(BF16)


def _in_proj(layer, x, b, s, g, w, gq, gk, gsg, cos, sup, sdn, wsg, bsg, gsgo, n, natural):
    d = g.shape[-1]
    nt = s // TILE
    table = pl.BlockSpec((n * TILE, LANES), lambda bi, i: (i, 0))
    spec = lambda arr, once=False: _layer_spec(arr, layer, once)
    out_specs = [
        _rows_spec(ATTN_WIDTH, n),
        pl.BlockSpec((1, 2, n * TILE, KV_WIDTH), lambda bi, i: (bi, 0, i, 0)),
        pl.BlockSpec((1, 4, n * TILE, KV_WIDTH), lambda bi, i: (bi, 0, i, 0)),
        _rows_spec(SG_WIDTH, n),
    ]
    out_shape = [
        jax.ShapeDtypeStruct((b, s, ATTN_WIDTH), BF16),
        jax.ShapeDtypeStruct((b, 2, s, KV_WIDTH), BF16),
        jax.ShapeDtypeStruct((b, 4, s, KV_WIDTH), BF16),
        jax.ShapeDtypeStruct((b, s, SG_WIDTH), BF16),
    ]
    scratch = [pltpu.VMEM((n * TILE, SG_WIDTH), F32)]
    if natural:
        x_spec = _rows_spec(d, n)
        out_specs.append(_tile_spec(d, nt, n))
        out_shape.append(jax.ShapeDtypeStruct((b * nt, SUBLANES, GROUPS * d), F32))
        scratch.append(pltpu.VMEM((d // LANES, n * TILE, LANES), F32))
    else:
        x_spec = _tile_spec(d, nt, n)
    return pl.pallas_call(
        _in_proj_kernel,
        grid=(b, nt // n),
        in_specs=[
            x_spec, spec(g), spec(w, True), spec(gq), spec(gk), spec(gsg),
            table, table, table, spec(wsg, True), spec(bsg, True), spec(gsgo),
        ],
        out_specs=out_specs,
        out_shape=out_shape,
        scratch_shapes=scratch,
        compiler_params=_PARAMS,
        name="in_proj",
    )(x, g, w, gq, gk, gsg, cos, sup, sdn, wsg, bsg, gsgo)


MAX_UNSHIFTED_SCORE = 40.0


def _attention_finish(slot, g_ref, x_ref, sg_ref, wo_ref, o_ref, o_scr):
    d = x_ref.shape[2] // GROUPS
    a_n = _rms(o_scr[slot], g_ref[...]).astype(BF16)
    y = (jnp.dot(a_n, wo_ref[:ATTN_WIDTH, :], preferred_element_type=F32)
         + jnp.dot(sg_ref[0], wo_ref[ATTN_WIDTH:, :], preferred_element_type=F32))
    _store_tile(o_ref, _load_tile(x_ref, d) + y, d)


def _attention_body(q_ref, k_ref, v_ref, g_ref, x_ref, sg_ref, wo_ref, o_ref, o_scr, shift):
    slot = pl.program_id(1) % 2
    lo = _low_half((TILE, LANES))
    heads_per_kv = N_HEADS // N_KV_HEADS

    def scores(h):
        c, half = divmod(h, 2)
        qb = q_ref[0, :, c * LANES:(c + 1) * LANES]
        keep = lo if half == 0 else jnp.logical_not(lo)
        qm = jnp.where(keep, qb, jnp.zeros_like(qb))
        kvar = 0 if h // heads_per_kv == half else 1
        return lax.dot_general(qm, k_ref[0, kvar], (((1,), (1,)), ((), ())),
                               preferred_element_type=F32)

    s = scores(0)
    _attention_finish(1 - slot, g_ref, x_ref, sg_ref, wo_ref, o_ref, o_scr)
    halves = []
    for h in range(N_HEADS):
        s_next = scores(h + 1) if h + 1 < N_HEADS else None
        c, half = divmod(h, 2)
        if shift:
            s = s - jnp.max(s, axis=-1, keepdims=True)
        p = jnp.exp2(s).astype(BF16)
        a = jnp.dot(p, v_ref[0, 2 * (h // heads_per_kv) + half], preferred_element_type=F32)
        halves.append(a / pltpu.roll(a, HALF, 1))
        if half == 1:
            o_scr[slot, :, c * LANES:(c + 1) * LANES] = jnp.where(lo, halves[0], halves[1])
            halves = []
        s = s_next


def _attention_kernel(bound_ref, q_ref, k_ref, v_ref, g_ref, x_ref, sg_ref, wo_ref, o_ref, o_scr):
    i = pl.program_id(1)
    n_tiles = pl.num_programs(1) - 1

    @pl.when(i == 0)
    def _():
        o_scr[...] = jnp.zeros_like(o_scr)

    unshifted_ok = bound_ref[0] <= MAX_UNSHIFTED_SCORE
    has_tile = i < n_tiles
    args = (q_ref, k_ref, v_ref, g_ref, x_ref, sg_ref, wo_ref, o_ref, o_scr)
    pl.when(jnp.logical_and(has_tile, unshifted_ok))(
        lambda: _attention_body(*args, shift=False))
    pl.when(jnp.logical_and(has_tile, jnp.logical_not(unshifted_ok)))(
        lambda: _attention_body(*args, shift=True))
    pl.when(jnp.logical_not(has_tile))(
        lambda: _attention_finish(1 - i % 2, g_ref, x_ref, sg_ref, wo_ref, o_ref, o_scr))


def _attention(layer, score_bound, q, k2, v4, g, xs, sg, w_o):
    b, s, _ = q.shape
    d = w_o.shape[-1]
    nt = s // TILE
    prev_tile = lambda i: jnp.maximum(i - 1, 0)
    return pl.pallas_call(
        _attention_kernel,
        grid=(b, nt + 1),
        in_specs=[
            pl.BlockSpec(memory_space=pltpu.SMEM),
            pl.BlockSpec((1, TILE, ATTN_WIDTH), lambda bi, i: (bi, jnp.minimum(i, nt - 1), 0)),
            pl.BlockSpec((1, 2, s, KV_WIDTH), lambda bi, i: (bi, 0, 0, 0)),
            pl.BlockSpec((1, 4, s, KV_WIDTH), lambda bi, i: (bi, 0, 0, 0)),
            _layer_spec(g, layer),
            pl.BlockSpec((1, SUBLANES, GROUPS * d), lambda bi, i: (bi * nt + prev_tile(i), 0, 0)),
            pl.BlockSpec((1, TILE, SG_WIDTH), lambda bi, i: (bi, prev_tile(i), 0)),
            _layer_spec(w_o, layer, True),
        ],
        out_specs=pl.BlockSpec((1, SUBLANES, GROUPS * d),
                               lambda bi, i: (bi * nt + prev_tile(i), 0, 0)),
        out_shape=jax.ShapeDtypeStruct(xs.shape, F32),
        scratch_shapes=[pltpu.VMEM((2, TILE, ATTN_WIDTH), F32)],
        compiler_params=pltpu.CompilerParams(dimension_semantics=("parallel", "arbitrary"),
                                             vmem_limit_bytes=VMEM_LIMIT),
        name="attention",
    )(score_bound, q, k2, v4, g, xs, sg, w_o)


def _ffn_kernel(x_ref, xp_ref, xn_ref, g_ref, wup_ref, cw_ref, cb_ref, wdn_ref, o_ref, *slab):
    n_tiles = x_ref.shape[0]
    d = g_ref.shape[1]
    for t in range(n_tiles):
        acc = _ffn_tile(t, n_tiles, x_ref, xp_ref, xn_ref, g_ref, wup_ref, cw_ref, cb_ref, wdn_ref)
        if not slab:
            _store_tile(o_ref, acc, d, t)
            continue
        for k in range(GROUPS):
            for j in range(d // LANES):
                slab[0][j, _strided_rows(t, k), :] = acc[k * SUBLANES:(k + 1) * SUBLANES,
                                                         j * LANES:(j + 1) * LANES]
        for j in range(d // LANES):
            o_ref[0, t * TILE:(t + 1) * TILE, j * LANES:(j + 1) * LANES] = (
                slab[0][j, t * TILE:(t + 1) * TILE])


def _ffn_tile(t, n_tiles, x_ref, xp_ref, xn_ref, g_ref, wup_ref, cw_ref, cb_ref, wdn_ref):
    i = pl.program_id(1)
    d = g_ref.shape[1]
    n_chunks, cf = wdn_ref.shape[0], wdn_ref.shape[1]
    g = g_ref[...]
    xt = _load_tile(x_ref, d, t)
    if t > 0:
        prev = _rms(x_ref[t - 1, :, (GROUPS - 1) * d:], g)
    else:
        prev = _rms(xp_ref[0], g) * (i > 0).astype(F32)
    if t + 1 < n_tiles:
        nxt = _rms(x_ref[t + 1, :, :d], g)
    else:
        nxt = _rms(xn_ref[0], g) * (i < pl.num_programs(1) - 1).astype(F32)
    xe = jnp.concatenate([prev, _rms(xt, g), nxt], axis=0).astype(BF16)
    cols = lambda j: slice(j * cf, (j + 1) * cf)

    def up_proj(j):
        return tuple(jnp.dot(xe, wup_ref[:, cols(col)], preferred_element_type=F32)
                     for col in (j, n_chunks + j))

    def conv(he, col):
        w = cw_ref[:, cols(col)]
        h = he[SUBLANES:SUBLANES + TILE]
        sub = lax.broadcasted_iota(jnp.int32, (SUBLANES, he.shape[1]), 0)
        first_prev = jnp.where(sub == 0, pltpu.roll(he[:SUBLANES], 1, 0),
                               pltpu.roll(h[TILE - SUBLANES:], 1, 0))
        last_next = jnp.where(sub == SUBLANES - 1,
                              pltpu.roll(he[SUBLANES + TILE:], SUBLANES - 1, 0),
                              pltpu.roll(h[:SUBLANES], SUBLANES - 1, 0))
        h_prev = jnp.concatenate([first_prev, h[:TILE - SUBLANES]], axis=0)
        h_next = jnp.concatenate([h[SUBLANES:], last_next], axis=0)
        return h_prev * w[0:1] + h * w[1:2] + h_next * w[2:3] + cb_ref[:, cols(col)]

    acc = xt
    h = up_proj(0)
    for j in range(n_chunks):
        h_next = up_proj(j + 1) if j + 1 < n_chunks else None
        gate = conv(h[0], j)
        up = conv(h[1], n_chunks + j)
        act = (gate * jax.nn.sigmoid(gate) * up).astype(BF16)
        acc = acc + jnp.dot(act, wdn_ref[j], preferred_element_type=F32)
        h = h_next
    return acc


def _ffn(layer, xs, b, s, g, wup, cw, cb, wdn, n, natural_out):
    d = g.shape[-1]
    nt = s // TILE
    spec = lambda arr, once=False: _layer_spec(arr, layer, once)
    if natural_out:
        out_spec, out_shape = _rows_spec(d, n), jax.ShapeDtypeStruct((b, s, d), F32)
        scratch = [pltpu.VMEM((d // LANES, n * TILE, LANES), F32)]
    else:
        out_spec, out_shape = _tile_spec(d, nt, n), jax.ShapeDtypeStruct(xs.shape, F32)
        scratch = []
    return pl.pallas_call(
        _ffn_kernel,
        grid=(b, nt // n),
        in_specs=[
            _tile_spec(d, nt, n),
            pl.BlockSpec((1, SUBLANES, d),
                         lambda bi, i: (bi * nt + jnp.maximum(i * n - 1, 0), 0, GROUPS - 1)),
            pl.BlockSpec((1, SUBLANES, d),
                         lambda bi, i: (bi * nt + jnp.minimum((i + 1) * n, nt - 1), 0, 0)),
            spec(g), spec(wup, True), spec(cw, True), spec(cb, True), spec(wdn, True),
        ],
        out_specs=out_spec,
        out_shape=out_shape,
        scratch_shapes=scratch,
        compiler_params=_PARAMS,
        name="ffn",
    )(xs, xs, xs, g, wup, cw, cb, wdn)


def _to_tile_order(t):
    s = t.shape[0]
    t = t.reshape((s // TILE, SUBLANES, GROUPS) + t.shape[1:])
    return jnp.swapaxes(t, 1, 2).reshape((s,) + t.shape[3:])


def _rope_tables(seq_len):
    rows = seq_len // GRID_W
    row = jnp.repeat(jnp.arange(rows, dtype=F32), GRID_W)
    col = jnp.tile(jnp.arange(GRID_W, dtype=F32), rows)
    axis_dim = HEAD_DIM // 2
    inv_freq = 1.0 / (ROPE_THETA ** (jnp.arange(0, axis_dim, 2, dtype=F32) / axis_dim))
    ang_r = row[:, None] * inv_freq[None, :]
    ang_c = col[:, None] * inv_freq[None, :]
    cos = jnp.concatenate([jnp.cos(ang_r)] * 2 + [jnp.cos(ang_c)] * 2, axis=-1)
    sin = jnp.concatenate([jnp.sin(ang_r)] * 2 + [jnp.sin(ang_c)] * 2, axis=-1)
    first = (jnp.arange(HEAD_DIM) % 32) < 16
    sin_up = jnp.where(first, -sin, 0.0)
    sin_dn = jnp.where(first, 0.0, sin)
    tile2 = lambda t: _to_tile_order(jnp.concatenate([t, t], axis=-1))
    return tile2(cos), tile2(sin_up), tile2(sin_dn)


def _gating_params(sg_w, sg_b):
    depth = sg_w.shape[0]
    per_chunk = CHUNK // GROUPS
    n_chunk = TILE // CHUNK
    assert per_chunk * GROUPS == CHUNK and per_chunk * n_chunk == SUBLANES
    w = sg_w.reshape(depth, SG_HEADS, per_chunk, GROUPS, per_chunk, GROUPS)
    w = w.transpose(0, 1, 3, 2, 5, 4)
    eye = jnp.eye(n_chunk, dtype=sg_w.dtype)
    w = w[:, :, :, None, :, :, None, :] * eye[None, None, None, :, None, None, :, None]
    wsg = w.reshape(depth, SG_HEADS, TILE, TILE).astype(BF16)
    bias = sg_b.reshape(depth, SG_HEADS, per_chunk, GROUPS).transpose(0, 1, 3, 2)
    bias = jnp.broadcast_to(bias[:, :, :, None, :],
                            (depth, SG_HEADS, GROUPS, n_chunk, per_chunk)).reshape(
                                depth, SG_HEADS // 2, 2, TILE)
    bsg = jnp.repeat(bias.transpose(0, 1, 3, 2), HALF, axis=-1)
    return wsg, bsg


def kernel(x, attn_norm_g, w_in, q_norm_g, k_norm_g, sg_norm_g, sg_w, sg_b, attn_out_g, sg_out_g,
           w_o, ffn_norm_g, w_up, conv_w, conv_b, w_down):
    b, s, d = x.shape
    depth = w_in.shape[0]
    d_ff = w_down.shape[1]
    assert s % TILE == 0 and s % GRID_W == 0 and d_ff % FF_CHUNK == 0
    nt = s // TILE
    n = 2 if nt % 2 == 0 else 1

    cos, sup, sdn = _rope_tables(s)
    scale = np.float32(HEAD_DIM ** -0.5 * np.log2(np.e))
    gq = jnp.tile(q_norm_g * scale, (1, N_HEADS))[:, None, :]
    score_bound = (HEAD_DIM * scale) * (jnp.max(jnp.abs(q_norm_g), axis=-1)
                                        * jnp.max(jnp.abs(k_norm_g), axis=-1))
    gk = jnp.tile(k_norm_g, (1, N_KV_HEADS))[:, None, :]
    gsg = sg_norm_g.reshape(depth, 1, SG_WIDTH)
    wsg, bsg = _gating_params(sg_w, sg_b)
    row = lambda p: p[:, None, :]
    w_in_b = w_in.astype(BF16)
    w_o_b = w_o.astype(BF16)
    wup = w_up.astype(BF16)
    wdn = w_down.astype(BF16).reshape(depth, d_ff // FF_CHUNK, FF_CHUNK, d)

    xs = x
    for i in range(depth):
        outs = _in_proj(i, xs, b, s, row(attn_norm_g), w_in_b, gq, gk, gsg,
                        cos, sup, sdn, wsg, bsg, row(sg_out_g), n, natural=(i == 0))
        q, k2, v4, sg = outs[:4]
        if i == 0:
            xs = outs[4]
        xs = _attention(i, score_bound[i:i + 1], q, k2, v4, row(attn_out_g), xs, sg, w_o_b)
        xs = _ffn(i, xs, b, s, row(ffn_norm_g), wup, conv_w, row(conv_b), wdn, n,
                  natural_out=(i == depth - 1))
    return xs
```

```python
import jax
import jax.numpy as jnp
import numpy as np
from jax import lax
from jax.experimental import pallas as pl
from jax.experimental.pallas import tpu as pltpu

GRID_W = 64
CHUNK = 128
N_HEADS = 8
N_KV_HEADS = 2
HEAD_DIM = 64
ATTN_WIDTH = N_HEADS * HEAD_DIM
KV_WIDTH = N_KV_HEADS * HEAD_DIM
SG_HEADS = 8
SG_WIDTH = SG_HEADS * HEAD_DIM
ROPE_THETA = 10000.0
EPS = 1e-6

LANES = 128
SUBLANES = 8
HALF = LANES // 2
TILE = 256
GROUPS = TILE // SUBLANES
FF_CHUNK = 256
VMEM_LIMIT = 56 * 1024 * 1024

F32 = jnp.float32
BF16 = jnp.bfloat16


def _rms(x, g):
    ms = jnp.mean(x * x, axis=-1, keepdims=True)
    return x * lax.rsqrt(ms + EPS) * g


def _low_half(shape):
    return lax.broadcasted_iota(jnp.int32, shape, len(shape) - 1) < HALF


def _head_rsqrt(xb):
    lo = _low_half(xb.shape)
    sq = xb * xb
    s_lo = jnp.sum(jnp.where(lo, sq, 0.0), axis=-1, keepdims=True)
    s_hi = jnp.sum(jnp.where(lo, 0.0, sq), axis=-1, keepdims=True)
    return lax.rsqrt(jnp.where(lo, s_lo, s_hi) * (1.0 / HEAD_DIM) + EPS)


def _gelu_tanh(x):
    c = np.float32(np.sqrt(2.0 / np.pi))
    return x * (0.5 * (1.0 + jnp.tanh(c * (x + 0.044715 * (x * x * x)))))


def _rope(xb, cos, sin_up, sin_dn):
    return xb * cos + pltpu.roll(xb, LANES - 16, 1) * sin_up + pltpu.roll(xb, 16, 1) * sin_dn


def _load_tile(x_ref, d, t=0):
    return jnp.concatenate([x_ref[t, :, k * d:(k + 1) * d] for k in range(GROUPS)], axis=0)


def _store_tile(o_ref, val, d, t=0):
    for k in range(GROUPS):
        o_ref[t, :, k * d:(k + 1) * d] = val[k * SUBLANES:(k + 1) * SUBLANES]


def _tile_spec(d, nt, n):
    return pl.BlockSpec((n, SUBLANES, GROUPS * d), lambda bi, i: (bi * (nt // n) + i, 0, 0))


def _rows_spec(width, n):
    return pl.BlockSpec((1, n * TILE, width), lambda bi, i: (bi, i, 0))


def _layer_spec(arr, layer, single_buffer=False):
    mode = dict(pipeline_mode=pl.Buffered(1)) if single_buffer else {}
    zeros = (0,) * (arr.ndim - 1)
    return pl.BlockSpec((None,) + arr.shape[1:], lambda bi, i: (layer,) + zeros, **mode)


_PARAMS = pltpu.CompilerParams(dimension_semantics=("parallel", "parallel"),
                               vmem_limit_bytes=VMEM_LIMIT)


def _strided_rows(t, k):
    return pl.ds(t * TILE + k, SUBLANES, stride=GROUPS)


def _load_natural_tile(x_ref, xs_ref, slab, d, t):
    pieces = []
    for k in range(GROUPS):
        piece = jnp.concatenate([slab[j, _strided_rows(t, k), :] for j in range(d // LANES)], axis=1)
        xs_ref[t, :, k * d:(k + 1) * d] = piece
        pieces.append(piece)
    return jnp.concatenate(pieces, axis=0)


def _in_proj_kernel(x_ref, g_ref, w_ref, gq_ref, gk_ref, gsg_ref, cos_ref, sup_ref, sdn_ref,
                    wsg_ref, bsg_ref, gsgo_ref, q_ref, k_ref, v_ref, sg_ref, *rest):
    d = g_ref.shape[1]
    lo = _low_half((TILE, LANES))
    if len(rest) == 1:
        sgo_scr, = rest
        n_tiles = x_ref.shape[0]
        tiles = [_load_tile(x_ref, d, t) for t in range(n_tiles)]
    else:
        xs_ref, sgo_scr, slab = rest
        n_tiles = x_ref.shape[1] // TILE
        for j in range(d // LANES):
            slab[j] = x_ref[0, :, j * LANES:(j + 1) * LANES]
        tiles = [_load_natural_tile(x_ref, xs_ref, slab, d, t) for t in range(n_tiles)]
    projs = [jnp.dot(_rms(xt, g_ref[...]).astype(BF16), w_ref[...], preferred_element_type=F32)
             for xt in tiles]
    for t, proj in enumerate(projs):
        rows = slice(t * TILE, (t + 1) * TILE)
        cos, sup, sdn = cos_ref[rows], sup_ref[rows], sdn_ref[rows]

        for c in range(ATTN_WIDTH // LANES):
            qb = proj[:, c * LANES:(c + 1) * LANES]
            qb = qb * _head_rsqrt(qb) * gq_ref[:, c * LANES:(c + 1) * LANES]
            q_ref[0, rows, c * LANES:(c + 1) * LANES] = _rope(qb, cos, sup, sdn).astype(BF16)

        kb = proj[:, ATTN_WIDTH:ATTN_WIDTH + KV_WIDTH]
        kb = _rope(kb * _head_rsqrt(kb) * gk_ref[...], cos, sup, sdn)
        k_ref[0, 0, rows] = kb.astype(BF16)
        k_ref[0, 1, rows] = pltpu.roll(kb, HALF, 1).astype(BF16)
        vb = proj[:, ATTN_WIDTH + KV_WIDTH:ATTN_WIDTH + 2 * KV_WIDTH]
        vs = pltpu.roll(vb, HALF, 1)
        v_ref[0, 0, rows] = jnp.where(lo, vb, 1.0).astype(BF16)
        v_ref[0, 1, rows] = jnp.where(lo, 1.0, vs).astype(BF16)
        v_ref[0, 2, rows] = jnp.where(lo, vs, 1.0).astype(BF16)
        v_ref[0, 3, rows] = jnp.where(lo, 1.0, vb).astype(BF16)

        u0 = ATTN_WIDTH + 2 * KV_WIDTH
        for c in range(SG_WIDTH // LANES):
            ub = _gelu_tanh(proj[:, u0 + c * LANES:u0 + (c + 1) * LANES])
            vb = _gelu_tanh(proj[:, u0 + SG_WIDTH + c * LANES:u0 + SG_WIDTH + (c + 1) * LANES])
            vb = vb * _head_rsqrt(vb) * gsg_ref[:, c * LANES:(c + 1) * LANES]
            mixed = (jnp.dot(wsg_ref[2 * c], jnp.where(lo, vb, 0.0).astype(BF16),
                             preferred_element_type=F32)
                     + jnp.dot(wsg_ref[2 * c + 1], jnp.where(lo, 0.0, vb).astype(BF16),
                               preferred_element_type=F32))
            sgo_scr[rows, c * LANES:(c + 1) * LANES] = ub * (mixed + bsg_ref[c])
        sg_ref[0, rows] = _rms(sgo_scr[rows], gsgo_ref[...]).astype(BF16)


def _in_proj(layer, x, b, s, g, w, gq, gk, gsg, cos, sup, sdn, wsg, bsg, gsgo, n, natural):
    d = g.shape[-1]
    nt = s // TILE
    table = pl.BlockSpec((n * TILE, LANES), lambda bi, i: (i, 0))
    spec = lambda arr, once=False: _layer_spec(arr, layer, once)
    out_specs = [
        _rows_spec(ATTN_WIDTH, n),
        pl.BlockSpec((1, 2, n * TILE, KV_WIDTH), lambda bi, i: (bi, 0, i, 0)),
        pl.BlockSpec((1, 4, n * TILE, KV_WIDTH), lambda bi, i: (bi, 0, i, 0)),
        _rows_spec(SG_WIDTH, n),
    ]
    out_shape = [
        jax.ShapeDtypeStruct((b, s, ATTN_WIDTH), BF16),
        jax.ShapeDtypeStruct((b, 2, s, KV_WIDTH), BF16),
        jax.ShapeDtypeStruct((b, 4, s, KV_WIDTH), BF16),
        jax.ShapeDtypeStruct((b, s, SG_WIDTH), BF16),
    ]
    scratch = [pltpu.VMEM((n * TILE, SG_WIDTH), F32)]
    if natural:
        x_spec = _rows_spec(d, n)
        out_specs.append(_tile_spec(d, nt, n))
        out_shape.append(jax.ShapeDtypeStruct((b * nt, SUBLANES, GROUPS * d), F32))
        scratch.append(pltpu.VMEM((d // LANES, n * TILE, LANES), F32))
    else:
        x_spec = _tile_spec(d, nt, n)
    return pl.pallas_call(
        _in_proj_kernel,
        grid=(b, nt // n),
        in_specs=[
            x_spec, spec(g), spec(w, True), spec(gq), spec(gk), spec(gsg),
            table, table, table, spec(wsg, True), spec(bsg, True), spec(gsgo),
        ],
        out_specs=out_specs,
        out_shape=out_shape,
        scratch_shapes=scratch,
        compiler_params=_PARAMS,
        name="in_proj",
    )(x, g, w, gq, gk, gsg, cos, sup, sdn, wsg, bsg, gsgo)


MAX_UNSHIFTED_SCORE = 40.0


def _attention_body(q_ref, k_ref, v_ref, g_ref, x_ref, sg_ref, wo_ref, o_ref, o_scr, shift):
    d = x_ref.shape[2] // GROUPS
    lo = _low_half((TILE, LANES))
    heads_per_kv = N_HEADS // N_KV_HEADS
    y_sg = jnp.dot(sg_ref[0], wo_ref[ATTN_WIDTH:, :], preferred_element_type=F32)

    def scores(h):
        c, half = divmod(h, 2)
        qb = q_ref[0, :, c * LANES:(c + 1) * LANES]
        keep = lo if half == 0 else jnp.logical_not(lo)
        qm = jnp.where(keep, qb, jnp.zeros_like(qb))
        kvar = 0 if h // heads_per_kv == half else 1
        return lax.dot_general(qm, k_ref[0, kvar], (((1,), (1,)), ((), ())),
                               preferred_element_type=F32)

    s = scores(0)
    halves = []
    for h in range(N_HEADS):
        s_next = scores(h + 1) if h + 1 < N_HEADS else None
        c, half = divmod(h, 2)
        if shift:
            s = s - jnp.max(s, axis=-1, keepdims=True)
        p = jnp.exp2(s).astype(BF16)
        a = jnp.dot(p, v_ref[0, 2 * (h // heads_per_kv) + half], preferred_element_type=F32)
        halves.append(a / pltpu.roll(a, HALF, 1))
        if half == 1:
            o_scr[:, c * LANES:(c + 1) * LANES] = jnp.where(lo, halves[0], halves[1])
            halves = []
        s = s_next
    a_n = _rms(o_scr[...], g_ref[...]).astype(BF16)
    y = jnp.dot(a_n, wo_ref[:ATTN_WIDTH, :], preferred_element_type=F32)
    _store_tile(o_ref, _load_tile(x_ref, d) + (y + y_sg), d)


def _attention_kernel(bound_ref, q_ref, k_ref, v_ref, g_ref, x_ref, sg_ref, wo_ref, o_ref, o_scr):
    unshifted_ok = bound_ref[0] <= MAX_UNSHIFTED_SCORE
    args = (q_ref, k_ref, v_ref, g_ref, x_ref, sg_ref, wo_ref, o_ref, o_scr)
    pl.when(unshifted_ok)(lambda: _attention_body(*args, shift=False))
    pl.when(jnp.logical_not(unshifted_ok))(lambda: _attention_body(*args, shift=True))


def _attention(layer, score_bound, q, k2, v4, g, xs, sg, w_o):
    b, s, _ = q.shape
    d = w_o.shape[-1]
    nt = s // TILE
    return pl.pallas_call(
        _attention_kernel,
        grid=(b, nt),
        in_specs=[
            pl.BlockSpec(memory_space=pltpu.SMEM),
            _rows_spec(ATTN_WIDTH, 1),
            pl.BlockSpec((1, 2, s, KV_WIDTH), lambda bi, i: (bi, 0, 0, 0)),
            pl.BlockSpec((1, 4, s, KV_WIDTH), lambda bi, i: (bi, 0, 0, 0)),
            _layer_spec(g, layer), _tile_spec(d, nt, 1), _rows_spec(SG_WIDTH, 1),
            _layer_spec(w_o, layer, True),
        ],
        out_specs=_tile_spec(d, nt, 1),
        out_shape=jax.ShapeDtypeStruct(xs.shape, F32),
        scratch_shapes=[pltpu.VMEM((TILE, ATTN_WIDTH), F32)],
        compiler_params=_PARAMS,
        name="attention",
    )(score_bound, q, k2, v4, g, xs, sg, w_o)


def _ffn_kernel(x_ref, xp_ref, xn_ref, g_ref, wup_ref, cw_ref, cb_ref, wdn_ref, o_ref, *slab):
    n_tiles = x_ref.shape[0]
    d = g_ref.shape[1]
    for t in range(n_tiles):
        acc = _ffn_tile(t, n_tiles, x_ref, xp_ref, xn_ref, g_ref, wup_ref, cw_ref, cb_ref, wdn_ref)
        if not slab:
            _store_tile(o_ref, acc, d, t)
            continue
        for k in range(GROUPS):
            for j in range(d // LANES):
                slab[0][j, _strided_rows(t, k), :] = acc[k * SUBLANES:(k + 1) * SUBLANES,
                                                         j * LANES:(j + 1) * LANES]
        for j in range(d // LANES):
            o_ref[0, t * TILE:(t + 1) * TILE, j * LANES:(j + 1) * LANES] = (
                slab[0][j, t * TILE:(t + 1) * TILE])


def _ffn_tile(t, n_tiles, x_ref, xp_ref, xn_ref, g_ref, wup_ref, cw_ref, cb_ref, wdn_ref):
    i = pl.program_id(1)
    d = g_ref.shape[1]
    n_chunks, cf = wdn_ref.shape[0], wdn_ref.shape[1]
    g = g_ref[...]
    xt = _load_tile(x_ref, d, t)
    if t > 0:
        prev = _rms(x_ref[t - 1, :, (GROUPS - 1) * d:], g)
    else:
        prev = _rms(xp_ref[0], g) * (i > 0).astype(F32)
    if t + 1 < n_tiles:
        nxt = _rms(x_ref[t + 1, :, :d], g)
    else:
        nxt = _rms(xn_ref[0], g) * (i < pl.num_programs(1) - 1).astype(F32)
    xe = jnp.concatenate([prev, _rms(xt, g), nxt], axis=0).astype(BF16)
    cols = lambda j: slice(j * cf, (j + 1) * cf)

    def up_proj(j):
        return tuple(jnp.dot(xe, wup_ref[:, cols(col)], preferred_element_type=F32)
                     for col in (j, n_chunks + j))

    def conv(he, col):
        w = cw_ref[:, cols(col)]
        h = he[SUBLANES:SUBLANES + TILE]
        sub = lax.broadcasted_iota(jnp.int32, (SUBLANES, he.shape[1]), 0)
        first_prev = jnp.where(sub == 0, pltpu.roll(he[:SUBLANES], 1, 0),
                               pltpu.roll(h[TILE - SUBLANES:], 1, 0))
        last_next = jnp.where(sub == SUBLANES - 1,
                              pltpu.roll(he[SUBLANES + TILE:], SUBLANES - 1, 0),
                              pltpu.roll(h[:SUBLANES], SUBLANES - 1, 0))
        h_prev = jnp.concatenate([first_prev, h[:TILE - SUBLANES]], axis=0)
        h_next = jnp.concatenate([h[SUBLANES:], last_next], axis=0)
        return h_prev * w[0:1] + h * w[1:2] + h_next * w[2:3] + cb_ref[:, cols(col)]

    acc = xt
    hs = [up_proj(j) for j in range(min(2, n_chunks))]
    for j in range(n_chunks):
        if j + 2 < n_chunks:
            hs.append(up_proj(j + 2))
        gate = conv(hs[j][0], j)
        up = conv(hs[j][1], n_chunks + j)
        act = (gate * jax.nn.sigmoid(gate) * up).astype(BF16)
        acc = acc + jnp.dot(act, wdn_ref[j], preferred_element_type=F32)
    return acc


def _ffn(layer, xs, b, s, g, wup, cw, cb, wdn, n, natural_out):
    d = g.shape[-1]
    nt = s // TILE
    spec = lambda arr, once=False: _layer_spec(arr, layer, once)
    if natural_out:
        out_spec, out_shape = _rows_spec(d, n), jax.ShapeDtypeStruct((b, s, d), F32)
        scratch = [pltpu.VMEM((d // LANES, n * TILE, LANES), F32)]
    else:
        out_spec, out_shape = _tile_spec(d, nt, n), jax.ShapeDtypeStruct(xs.shape, F32)
        scratch = []
    return pl.pallas_call(
        _ffn_kernel,
        grid=(b, nt // n),
        in_specs=[
            _tile_spec(d, nt, n),
            pl.BlockSpec((1, SUBLANES, d),
                         lambda bi, i: (bi * nt + jnp.maximum(i * n - 1, 0), 0, GROUPS - 1)),
            pl.BlockSpec((1, SUBLANES, d),
                         lambda bi, i: (bi * nt + jnp.minimum((i + 1) * n, nt - 1), 0, 0)),
            spec(g), spec(wup, True), spec(cw, True), spec(cb, True), spec(wdn, True),
        ],
        out_specs=out_spec,
        out_shape=out_shape,
        scratch_shapes=scratch,
        compiler_params=_PARAMS,
        name="ffn",
    )(xs, xs, xs, g, wup, cw, cb, wdn)


def _to_tile_order(t):
    s = t.shape[0]
    t = t.reshape((s // TILE, SUBLANES, GROUPS) + t.shape[1:])
    return jnp.swapaxes(t, 1, 2).reshape((s,) + t.shape[3:])


def _rope_tables(seq_len):
    rows = seq_len // GRID_W
    row = jnp.repeat(jnp.arange(rows, dtype=F32), GRID_W)
    col = jnp.tile(jnp.arange(GRID_W, dtype=F32), rows)
    axis_dim = HEAD_DIM // 2
    inv_freq = 1.0 / (ROPE_THETA ** (jnp.arange(0, axis_dim, 2, dtype=F32) / axis_dim))
    ang_r = row[:, None] * inv_freq[None, :]
    ang_c = col[:, None] * inv_freq[None, :]
    cos = jnp.concatenate([jnp.cos(ang_r)] * 2 + [jnp.cos(ang_c)] * 2, axis=-1)
    sin = jnp.concatenate([jnp.sin(ang_r)] * 2 + [jnp.sin(ang_c)] * 2, axis=-1)
    first = (jnp.arange(HEAD_DIM) % 32) < 16
    sin_up = jnp.where(first, -sin, 0.0)
    sin_dn = jnp.where(first, 0.0, sin)
    tile2 = lambda t: _to_tile_order(jnp.concatenate([t, t], axis=-1))
    return tile2(cos), tile2(sin_up), tile2(sin_dn)


def _gating_params(sg_w, sg_b):
    depth = sg_w.shape[0]
    per_chunk = CHUNK // GROUPS
    n_chunk = TILE // CHUNK
    assert per_chunk * GROUPS == CHUNK and per_chunk * n_chunk == SUBLANES
    w = sg_w.reshape(depth, SG_HEADS, per_chunk, GROUPS, per_chunk, GROUPS)
    w = w.transpose(0, 1, 3, 2, 5, 4)
    eye = jnp.eye(n_chunk, dtype=sg_w.dtype)
    w = w[:, :, :, None, :, :, None, :] * eye[None, None, None, :, None, None, :, None]
    wsg = w.reshape(depth, SG_HEADS, TILE, TILE).astype(BF16)
    bias = sg_b.reshape(depth, SG_HEADS, per_chunk, GROUPS).transpose(0, 1, 3, 2)
    bias = jnp.broadcast_to(bias[:, :, :, None, :],
                            (depth, SG_HEADS, GROUPS, n_chunk, per_chunk)).reshape(
                                depth, SG_HEADS // 2, 2, TILE)
    bsg = jnp.repeat(bias.transpose(0, 1, 3, 2), HALF, axis=-1)
    return wsg, bsg


def kernel(x, attn_norm_g, w_in, q_norm_g, k_norm_g, sg_norm_g, sg_w, sg_b, attn_out_g, sg_out_g,
           w_o, ffn_norm_g, w_up, conv_w, conv_b, w_down):
    b, s, d = x.shape
    depth = w_in.shape[0]
    d_ff = w_down.shape[1]
    assert s % TILE == 0 and s % GRID_W == 0 and d_ff % FF_CHUNK == 0
    nt = s // TILE
    n = 2 if nt % 2 == 0 else 1

    cos, sup, sdn = _rope_tables(s)
    scale = np.float32(HEAD_DIM ** -0.5 * np.log2(np.e))
    gq = jnp.tile(q_norm_g * scale, (1, N_HEADS))[:, None, :]
    score_bound = (HEAD_DIM * scale) * (jnp.max(jnp.abs(q_norm_g), axis=-1)
                                        * jnp.max(jnp.abs(k_norm_g), axis=-1))
    gk = jnp.tile(k_norm_g, (1, N_KV_HEADS))[:, None, :]
    gsg = sg_norm_g.reshape(depth, 1, SG_WIDTH)
    wsg, bsg = _gating_params(sg_w, sg_b)
    row = lambda p: p[:, None, :]
    w_in_b = w_in.astype(BF16)
    w_o_b = w_o.astype(BF16)
    wup = w_up.astype(BF16)
    wdn = w_down.astype(BF16).reshape(depth, d_ff // FF_CHUNK, FF_CHUNK, d)

    xs = x
    for i in range(depth):
        outs = _in_proj(i, xs, b, s, row(attn_norm_g), w_in_b, gq, gk, gsg,
                        cos, sup, sdn, wsg, bsg, row(sg_out_g), n, natural=(i == 0))
        q, k2, v4, sg = outs[:4]
        if i == 0:
            xs = outs[4]
        xs = _attention(i, score_bound[i:i + 1], q, k2, v4, row(attn_out_g), xs, sg, w_o_b)
        xs = _ffn(i, xs, b, s, row(ffn_norm_g), wup, conv_w, row(conv_b), wdn, n,
                  natural_out=(i == depth - 1))
    return xs
```

```python
import jax
import jax.numpy as jnp
import numpy as np
from jax import lax
from jax.experimental import pallas as pl
from jax.experimental.pallas import tpu as pltpu

GRID_W = 64
CHUNK = 128
N_HEADS = 8
N_KV_HEADS = 2
HEAD_DIM = 64
ATTN_WIDTH = N_HEADS * HEAD_DIM
KV_WIDTH = N_KV_HEADS * HEAD_DIM
SG_HEADS = 8
SG_WIDTH = SG_HEADS * HEAD_DIM
ROPE_THETA = 10000.0
EPS = 1e-6

LANES = 128
SUBLANES = 8
HALF = LANES // 2
TILE = 256
GROUPS = TILE // SUBLANES
FF_CHUNK = 256
VMEM_LIMIT = 56 * 1024 * 1024

F32 = jnp.float32
BF16 = jnp.bfloat16


def _rms(x, g):
    ms = jnp.mean(x * x, axis=-1, keepdims=True)
    return x * lax.rsqrt(ms + EPS) * g


def _low_half(shape):
    return lax.broadcasted_iota(jnp.int32, shape, len(shape) - 1) < HALF


def _head_rsqrt(xb):
    lo = _low_half(xb.shape)
    sq = xb * xb
    s_lo = jnp.sum(jnp.where(lo, sq, 0.0), axis=-1, keepdims=True)
    s_hi = jnp.sum(jnp.where(lo, 0.0, sq), axis=-1, keepdims=True)
    return lax.rsqrt(jnp.where(lo, s_lo, s_hi) * (1.0 / HEAD_DIM) + EPS)


def _gelu_tanh(x):
    c = np.float32(np.sqrt(2.0 / np.pi))
    return x * (0.5 * (1.0 + jnp.tanh(c * (x + 0.044715 * (x * x * x)))))


def _rope(xb, cos, sin_up, sin_dn):
    return xb * cos + pltpu.roll(xb, LANES - 16, 1) * sin_up + pltpu.roll(xb, 16, 1) * sin_dn


def _load_tile(x_ref, d, t=0):
    return jnp.concatenate([x_ref[t, :, k * d:(k + 1) * d] for k in range(GROUPS)], axis=0)


def _store_tile(o_ref, val, d, t=0):
    for k in range(GROUPS):
        o_ref[t, :, k * d:(k + 1) * d] = val[k * SUBLANES:(k + 1) * SUBLANES]


def _tile_spec(d, nt, n):
    return pl.BlockSpec((n, SUBLANES, GROUPS * d), lambda bi, i: (bi * (nt // n) + i, 0, 0))


def _rows_spec(width, n):
    return pl.BlockSpec((1, n * TILE, width), lambda bi, i: (bi, i, 0))


def _layer_spec(arr, layer, single_buffer=False):
    mode = dict(pipeline_mode=pl.Buffered(1)) if single_buffer else {}
    zeros = (0,) * (arr.ndim - 1)
    return pl.BlockSpec((None,) + arr.shape[1:], lambda bi, i: (layer,) + zeros, **mode)


_PARAMS = pltpu.CompilerParams(dimension_semantics=("parallel", "parallel"),
                               vmem_limit_bytes=VMEM_LIMIT)


SLAB_PITCH = GROUPS + SUBLANES


def _slab_shape(d, n):
    return (d // LANES, n * SUBLANES * SLAB_PITCH, LANES)


def _slab_run(t, s):
    run = t * SUBLANES + s
    return slice(run * SLAB_PITCH, run * SLAB_PITCH + GROUPS), slice(run * GROUPS, (run + 1) * GROUPS)


def _strided_rows(t, k):
    return pl.ds(t * SUBLANES * SLAB_PITCH + k, SUBLANES, stride=SLAB_PITCH)


def _load_natural_tile(x_ref, xs_ref, slab, d, t):
    pieces = []
    for k in range(GROUPS):
        piece = jnp.concatenate([slab[j, _strided_rows(t, k), :] for j in range(d // LANES)], axis=1)
        xs_ref[t, :, k * d:(k + 1) * d] = piece
        pieces.append(piece)
    return jnp.concatenate(pieces, axis=0)


def _in_proj_kernel(x_ref, g_ref, w_ref, gq_ref, gk_ref, gsg_ref, cos_ref, sup_ref, sdn_ref,
                    wsg_ref, bsg_ref, gsgo_ref, q_ref, k_ref, v_ref, sg_ref, *rest):
    d = g_ref.shape[1]
    lo = _low_half((TILE, LANES))
    if len(rest) == 1:
        sgo_scr, = rest
        n_tiles = x_ref.shape[0]
        tiles = [_load_tile(x_ref, d, t) for t in range(n_tiles)]
    else:
        xs_ref, sgo_scr, slab = rest
        n_tiles = x_ref.shape[1] // TILE
        for t in range(n_tiles):
            for s in range(SUBLANES):
                slab_rows, rows = _slab_run(t, s)
                for j in range(d // LANES):
                    slab[j, slab_rows] = x_ref[0, rows, j * LANES:(j + 1) * LANES]
        tiles =[_load_natural_tile(x_ref, xs_ref, slab, d, t) for t in range(n_tiles)]
    projs = [jnp.dot(_rms(xt, g_ref[...]).astype(BF16), w_ref[...], preferred_element_type=F32)
             for xt in tiles]
    for t, proj in enumerate(projs):
        rows = slice(t * TILE, (t + 1) * TILE)
        cos, sup, sdn = cos_ref[rows], sup_ref[rows], sdn_ref[rows]

        for c in range(ATTN_WIDTH // LANES):
            qb = proj[:, c * LANES:(c + 1) * LANES]
            qb = qb * _head_rsqrt(qb) * gq_ref[:, c * LANES:(c + 1) * LANES]
            q_ref[0, rows, c * LANES:(c + 1) * LANES] = _rope(qb, cos, sup, sdn).astype(BF16)

        kb = proj[:, ATTN_WIDTH:ATTN_WIDTH + KV_WIDTH]
        kb = _rope(kb * _head_rsqrt(kb) * gk_ref[...], cos, sup, sdn)
        k_ref[0, 0, rows] = kb.astype(BF16)
        k_ref[0, 1, rows] = pltpu.roll(kb, HALF, 1).astype(BF16)
        vb = proj[:, ATTN_WIDTH + KV_WIDTH:ATTN_WIDTH + 2 * KV_WIDTH]
        vs = pltpu.roll(vb, HALF, 1)
        v_ref[0, 0, rows] = jnp.where(lo, vb, 1.0).astype(BF16)
        v_ref[0, 1, rows] = jnp.where(lo, 1.0, vs).astype(BF16)
        v_ref[0, 2, rows] = jnp.where(lo, vs, 1.0).astype(BF16)
        v_ref[0, 3, rows] = jnp.where(lo, 1.0, vb).astype(BF16)

        u0 = ATTN_WIDTH + 2 * KV_WIDTH
        for c in range(SG_WIDTH // LANES):
            ub = _gelu_tanh(proj[:, u0 + c * LANES:u0 + (c + 1) * LANES])
            vb = _gelu_tanh(proj[:, u0 + SG_WIDTH + c * LANES:u0 + SG_WIDTH + (c + 1) * LANES])
            vb = vb * _head_rsqrt(vb) * gsg_ref[:, c * LANES:(c + 1) * LANES]
            mixed = (jnp.dot(wsg_ref[2 * c], jnp.where(lo, vb, 0.0).astype(BF16),
                             preferred_element_type=F32)
                     + jnp.dot(wsg_ref[2 * c + 1], jnp.where(lo, 0.0, vb).astype(BF16),
                               preferred_element_type=F32))
            sgo_scr[rows, c * LANES:(c + 1) * LANES] = ub * (mixed + bsg_ref[c])
        sg_ref[0, rows] = _rms(sgo_scr[rows], gsgo_ref[...]).astype(BF16)


def _in_proj(layer, x, b, s, g, w, gq, gk, gsg, cos, sup, sdn, wsg, bsg, gsgo, n, natural):
    d = g.shape[-1]
    nt = s // TILE
    table = pl.BlockSpec((n * TILE, LANES), lambda bi, i: (i, 0))
    spec = lambda arr, once=False: _layer_spec(arr, layer, once)
    out_specs = [
        _rows_spec(ATTN_WIDTH, n),
        pl.BlockSpec((1, 2, n * TILE, KV_WIDTH), lambda bi, i: (bi, 0, i, 0)),
        pl.BlockSpec((1, 4, n * TILE, KV_WIDTH), lambda bi, i: (bi, 0, i, 0)),
        _rows_spec(SG_WIDTH, n),
    ]
    out_shape = [
        jax.ShapeDtypeStruct((b, s, ATTN_WIDTH), BF16),
        jax.ShapeDtypeStruct((b, 2, s, KV_WIDTH), BF16),
        jax.ShapeDtypeStruct((b, 4, s, KV_WIDTH), BF16),
        jax.ShapeDtypeStruct((b, s, SG_WIDTH), BF16),
    ]
    scratch = [pltpu.VMEM((n * TILE, SG_WIDTH), F32)]
    if natural:
        x_spec = _rows_spec(d, n)
        out_specs.append(_tile_spec(d, nt, n))
        out_shape.append(jax.ShapeDtypeStruct((b * nt, SUBLANES, GROUPS * d), F32))
        scratch.append(pltpu.VMEM(_slab_shape(d, n), F32))
    else:
        x_spec = _tile_spec(d, nt, n)
    return pl.pallas_call(
        _in_proj_kernel,
        grid=(b, nt // n),
        in_specs=[
            x_spec, spec(g), spec(w, True), spec(gq), spec(gk), spec(gsg),
            table, table, table, spec(wsg, True), spec(bsg, True), spec(gsgo),
        ],
        out_specs=out_specs,
        out_shape=out_shape,
        scratch_shapes=scratch,
        compiler_params=_PARAMS,
        name="in_proj",
    )(x, g, w, gq, gk, gsg, cos, sup, sdn, wsg, bsg, gsgo)


MAX_UNSHIFTED_SCORE = 40.0


def _attention_body(q_ref, k_ref, v_ref, g_ref, x_ref, sg_ref, wo_ref, o_ref, o_scr, shift):
    d = x_ref.shape[2] // GROUPS
    lo = _low_half((TILE, LANES))
    heads_per_kv = N_HEADS // N_KV_HEADS
    y_sg = jnp.dot(sg_ref[0], wo_ref[ATTN_WIDTH:, :], preferred_element_type=F32)

    def scores(h):
        c, half = divmod(h, 2)
        qb = q_ref[0, :, c * LANES:(c + 1) * LANES]
        keep = lo if half == 0 else jnp.logical_not(lo)
        qm = jnp.where(keep, qb, jnp.zeros_like(qb))
        kvar = 0 if h // heads_per_kv == half else 1
        return lax.dot_general(qm, k_ref[0, kvar], (((1,), (1,)), ((), ())),
                               preferred_element_type=F32)

    s = scores(0)
    halves = []
    for h in range(N_HEADS):
        s_next = scores(h + 1) if h + 1 < N_HEADS else None
        c, half = divmod(h, 2)
        if shift:
            s = s - jnp.max(s, axis=-1, keepdims=True)
        p = jnp.exp2(s).astype(BF16)
        a = jnp.dot(p, v_ref[0, 2 * (h // heads_per_kv) + half], preferred_element_type=F32)
        halves.append(a / pltpu.roll(a, HALF, 1))
        if half == 1:
            o_scr[:, c * LANES:(c + 1) * LANES] = jnp.where(lo, halves[0], halves[1])
            halves = []
        s = s_next
    a_n = _rms(o_scr[...], g_ref[...]).astype(BF16)
    y = jnp.dot(a_n, wo_ref[:ATTN_WIDTH, :], preferred_element_type=F32)
    _store_tile(o_ref, _load_tile(x_ref, d) + (y + y_sg), d)


def _attention_kernel(bound_ref, q_ref, k_ref, v_ref, g_ref, x_ref, sg_ref, wo_ref, o_ref, o_scr):
    unshifted_ok = bound_ref[0] <= MAX_UNSHIFTED_SCORE
    args = (q_ref, k_ref, v_ref, g_ref, x_ref, sg_ref, wo_ref, o_ref, o_scr)
    pl.when(unshifted_ok)(lambda: _attention_body(*args, shift=False))
    pl.when(jnp.logical_not(unshifted_ok))(lambda: _attention_body(*args, shift=True))


def _attention(layer, score_bound, q, k2, v4, g, xs, sg, w_o):
    b, s, _ = q.shape
    d = w_o.shape[-1]
    nt = s // TILE
    return pl.pallas_call(
        _attention_kernel,
        grid=(b, nt),
        in_specs=[
            pl.BlockSpec(memory_space=pltpu.SMEM),
            _rows_spec(ATTN_WIDTH, 1),
            pl.BlockSpec((1, 2, s, KV_WIDTH), lambda bi, i: (bi, 0, 0, 0)),
            pl.BlockSpec((1, 4, s, KV_WIDTH), lambda bi, i: (bi, 0, 0, 0)),
            _layer_spec(g, layer), _tile_spec(d, nt, 1), _rows_spec(SG_WIDTH, 1),
            _layer_spec(w_o, layer, True),
        ],
        out_specs=_tile_spec(d, nt, 1),
        out_shape=jax.ShapeDtypeStruct(xs.shape, F32),
        scratch_shapes=[pltpu.VMEM((TILE, ATTN_WIDTH), F32)],
        compiler_params=_PARAMS,
        name="attention",
    )(score_bound, q, k2, v4, g, xs, sg, w_o)


def _ffn_kernel(x_ref, xp_ref, xn_ref, g_ref, wup_ref, cw_ref, cb_ref, wdn_ref, o_ref, *slab):
    n_tiles = x_ref.shape[0]
    d = g_ref.shape[1]
    for t in range(n_tiles):
        acc = _ffn_tile(t, n_tiles, x_ref, xp_ref, xn_ref, g_ref, wup_ref, cw_ref, cb_ref, wdn_ref)
        if not slab:
            _store_tile(o_ref, acc, d, t)
            continue
        for k in range(GROUPS):
            for j in range(d // LANES):
                slab[0][j, _strided_rows(t, k), :] = acc[k * SUBLANES:(k + 1) * SUBLANES,
                                                         j * LANES:(j + 1) * LANES]
        for s in range(SUBLANES):
            slab_rows, rows = _slab_run(t, s)
            for j in range(d // LANES):
                o_ref[0, rows, j * LANES:(j + 1) * LANES] = slab[0][j, slab_rows]


def _ffn_tile(t, n_tiles, x_ref, xp_ref, xn_ref, g_ref, wup_ref, cw_ref, cb_ref, wdn_ref):
    i = pl.program_id(1)
    d = g_ref.shape[1]
    n_chunks, cf = wdn_ref.shape[0], wdn_ref.shape[1]
    g = g_ref[...]
    xt = _load_tile(x_ref, d, t)
    if t > 0:
        prev = _rms(x_ref[t - 1, :, (GROUPS - 1) * d:], g)
    else:
        prev = _rms(xp_ref[0], g) * (i > 0).astype(F32)
    if t + 1 < n_tiles:
        nxt = _rms(x_ref[t + 1, :, :d], g)
    else:
        nxt = _rms(xn_ref[0], g) * (i < pl.num_programs(1) - 1).astype(F32)
    xe = jnp.concatenate([prev, _rms(xt, g), nxt], axis=0).astype(BF16)
    cols = lambda j: slice(j * cf, (j + 1) * cf)

    def up_proj(j):
        return tuple(jnp.dot(xe, wup_ref[:, cols(col)], preferred_element_type=F32)
                     for col in (j, n_chunks + j))

    def conv(he, col):
        w = cw_ref[:, cols(col)]
        h = he[SUBLANES:SUBLANES + TILE]
        sub = lax.broadcasted_iota(jnp.int32, (SUBLANES, he.shape[1]), 0)
        first_prev = jnp.where(sub == 0, pltpu.roll(he[:SUBLANES], 1, 0),
                               pltpu.roll(h[TILE - SUBLANES:], 1, 0))
        last_next = jnp.where(sub == SUBLANES - 1,
                              pltpu.roll(he[SUBLANES + TILE:], SUBLANES - 1, 0),
                              pltpu.roll(h[:SUBLANES], SUBLANES - 1, 0))
        h_prev = jnp.concatenate([first_prev, h[:TILE - SUBLANES]], axis=0)
        h_next = jnp.concatenate([h[SUBLANES:], last_next], axis=0)
        return h_prev * w[0:1] + h * w[1:2] + h_next * w[2:3] + cb_ref[:, cols(col)]

    ahead = 3
    acc = xt
    hs = [up_proj(j) for j in range(min(ahead, n_chunks))]
    for j in range(n_chunks):
        if j + ahead < n_chunks:
            hs.append(up_proj(j + ahead))
        gate = conv(hs[j][0], j)
        up = conv(hs[j][1], n_chunks + j)
        act = (gate * jax.nn.sigmoid(gate) * up).astype(BF16)
        acc = acc + jnp.dot(act, wdn_ref[j], preferred_element_type=F32)
    return acc


def _ffn(layer, xs, b, s, g, wup, cw, cb, wdn, n, natural_out):
    d = g.shape[-1]
    nt = s // TILE
    spec = lambda arr, once=False: _layer_spec(arr, layer, once)
    if natural_out:
        out_spec, out_shape = _rows_spec(d, n), jax.ShapeDtypeStruct((b, s, d), F32)
        scratch = [pltpu.VMEM(_slab_shape(d, n), F32)]
    else:
        out_spec, out_shape = _tile_spec(d, nt, n), jax.ShapeDtypeStruct(xs.shape, F32)
        scratch = []
    return pl.pallas_call(
        _ffn_kernel,
        grid=(b, nt // n),
        in_specs=[
            _tile_spec(d, nt, n),
            pl.BlockSpec((1, SUBLANES, d),
                         lambda bi, i: (bi * nt + jnp.maximum(i * n - 1, 0), 0, GROUPS - 1)),
            pl.BlockSpec((1, SUBLANES, d),
                         lambda bi, i: (bi * nt + jnp.minimum((i + 1) * n, nt - 1), 0, 0)),
            spec(g), spec(wup, True), spec(cw, True), spec(cb, True), spec(wdn, True),
        ],
        out_specs=out_spec,
        out_shape=out_shape,
        scratch_shapes=scratch,
        compiler_params=_PARAMS,
        name="ffn",
    )(xs, xs, xs, g, wup, cw, cb, wdn)


def _to_tile_order(t):
    s = t.shape[0]
    t = t.reshape((s // TILE, SUBLANES, GROUPS) + t.shape[1:])
    return jnp.swapaxes(t, 1, 2).reshape((s,) + t.shape[3:])


def _rope_tables(seq_len):
    rows = seq_len // GRID_W
    row = jnp.repeat(jnp.arange(rows, dtype=F32), GRID_W)
    col = jnp.tile(jnp.arange(GRID_W, dtype=F32), rows)
    axis_dim = HEAD_DIM // 2
    inv_freq = 1.0 / (ROPE_THETA ** (jnp.arange(0, axis_dim, 2, dtype=F32) / axis_dim))
    ang_r = row[:, None] * inv_freq[None, :]
    ang_c = col[:, None] * inv_freq[None, :]
    cos = jnp.concatenate([jnp.cos(ang_r)] * 2 + [jnp.cos(ang_c)] * 2, axis=-1)
    sin = jnp.concatenate([jnp.sin(ang_r)] * 2 + [jnp.sin(ang_c)] * 2, axis=-1)
    first = (jnp.arange(HEAD_DIM) % 32) < 16
    sin_up = jnp.where(first, -sin, 0.0)
    sin_dn = jnp.where(first, 0.0, sin)
    tile2 = lambda t: _to_tile_order(jnp.concatenate([t, t], axis=-1))
    return tile2(cos), tile2(sin_up), tile2(sin_dn)


def _gating_params(sg_w, sg_b):
    depth = sg_w.shape[0]
    per_chunk = CHUNK // GROUPS
    n_chunk = TILE // CHUNK
    assert per_chunk * GROUPS == CHUNK and per_chunk * n_chunk == SUBLANES
    w = sg_w.reshape(depth, SG_HEADS, per_chunk, GROUPS, per_chunk, GROUPS)
    w = w.transpose(0, 1, 3, 2, 5, 4)
    eye = jnp.eye(n_chunk, dtype=sg_w.dtype)
    w = w[:, :, :, None, :, :, None, :] * eye[None, None, None, :, None, None, :, None]
    wsg = w.reshape(depth, SG_HEADS, TILE, TILE).astype(BF16)
    bias = sg_b.reshape(depth, SG_HEADS, per_chunk, GROUPS).transpose(0, 1, 3, 2)
    bias = jnp.broadcast_to(bias[:, :, :, None, :],
                            (depth, SG_HEADS, GROUPS, n_chunk, per_chunk)).reshape(
                                depth, SG_HEADS // 2, 2, TILE)
    bsg = jnp.repeat(bias.transpose(0, 1, 3, 2), HALF, axis=-1)
    return wsg, bsg


def kernel(x, attn_norm_g, w_in, q_norm_g, k_norm_g, sg_norm_g, sg_w, sg_b, attn_out_g, sg_out_g,
           w_o, ffn_norm_g, w_up, conv_w, conv_b, w_down):
    b, s, d = x.shape
    depth = w_in.shape[0]
    d_ff = w_down.shape[1]
    assert s % TILE == 0 and s % GRID_W == 0 and d_ff % FF_CHUNK == 0
    nt = s // TILE
    n = 2 if nt % 2 == 0 else 1

    cos, sup, sdn = _rope_tables(s)
    scale = np.float32(HEAD_DIM ** -0.5 * np.log2(np.e))
    gq = jnp.tile(q_norm_g * scale, (1, N_HEADS))[:, None, :]
    score_bound = (HEAD_DIM * scale) * (jnp.max(jnp.abs(q_norm_g), axis=-1)
                                        * jnp.max(jnp.abs(k_norm_g), axis=-1))
    gk = jnp.tile(k_norm_g, (1, N_KV_HEADS))[:, None, :]
    gsg = sg_norm_g.reshape(depth, 1, SG_WIDTH)
    wsg, bsg = _gating_params(sg_w, sg_b)
    row = lambda p: p[:, None, :]
    w_in_b = w_in.astype(BF16)
    w_o_b = w_o.astype(BF16)
    wup = w_up.astype(BF16)
    wdn = w_down.astype(BF16).reshape(depth, d_ff // FF_CHUNK, FF_CHUNK, d)

    xs = x
    for i in range(depth):
        outs = _in_proj(i, xs, b, s, row(attn_norm_g), w_in_b, gq, gk, gsg,
                        cos, sup, sdn, wsg, bsg, row(sg_out_g), n, natural=(i == 0))
        q, k2, v4, sg = outs[:4]
        if i == 0:
            xs = outs[4]
        xs = _attention(i, score_bound[i:i + 1], q, k2, v4, row(attn_out_g), xs, sg, w_o_b)
        xs = _ffn(i, xs, b, s, row(ffn_norm_g), wup, conv_w, row(conv_b), wdn, n,
                  natural_out=(i == depth - 1))
    return xs
```
